```python
import math
import jax, jax.numpy as jnp
from jax import lax
import numpy as np

D_MODEL = 1024
BATCH = 8
SEQ = 8192
DEPTH = 4

A_HEADS = 8
A_KV_HEADS = 2
A_HEAD_DIM = 64
WINDOW = 128
BLOCK = 128
B_HEADS = 8
Q_LORA = 384
KV_LORA = 256
NOPE_DIM = 64
ROPE_DIM = 32
V_DIM = 64
ROPE_THETA = 10000.0
N_BUCKETS = 32
MAX_DISTANCE = 128
D_FF = 4 * D_MODEL
EPS = 1e-5

A_Q_W = A_HEADS * A_HEAD_DIM
A_KV_W = A_KV_HEADS * A_HEAD_DIM
B_QK_DIM = NOPE_DIM + ROPE_DIM
A_OUT = A_HEADS * A_HEAD_DIM
B_OUT = B_HEADS * V_DIM
IN_SPLITS = (A_Q_W, A_KV_W, A_KV_W, Q_LORA, KV_LORA, ROPE_DIM, D_MODEL, D_MODEL)
D_IN = A_Q_W + 2 * A_KV_W + Q_LORA + KV_LORA + ROPE_DIM + 2 * D_MODEL

kernel_name = "hybrid_swa_sink_mla_gated_trunk"


def rmsnorm(x, g):
    xf = x.astype(jnp.float32)
    y = xf * lax.rsqrt(jnp.mean(xf * xf, axis=-1, keepdims=True) + EPS)
    return (y * g.astype(jnp.float32)).astype(x.dtype)


def rope(t, positions):
    half = ROPE_DIM // 2
    inv_freq = ROPE_THETA ** (-jnp.arange(half, dtype=jnp.float32) / half)
    ang = positions.astype(jnp.float32)[..., None] * inv_freq
    ang = ang.reshape(ang.shape[:2] + (1,) * (t.ndim - 3) + (half,))
    cos, sin = jnp.cos(ang), jnp.sin(ang)
    tf = t.astype(jnp.float32)
    t1, t2 = tf[..., :half], tf[..., half:]
    return jnp.concatenate([t1 * cos - t2 * sin, t2 * cos + t1 * sin], axis=-1).astype(t.dtype)


def t5_bucket(dist):
    max_exact = N_BUCKETS // 2
    n = jnp.maximum(dist, 0)
    nf = jnp.maximum(n, 1).astype(jnp.float32)
    large = max_exact + (jnp.log(nf / max_exact) / math.log(MAX_DISTANCE / max_exact)
                         * (N_BUCKETS - max_exact)).astype(jnp.int32)
    large = jnp.minimum(large, N_BUCKETS - 1)
    return jnp.where(n < max_exact, n, large)


def swa_sink_attention(q, k, v, sinks, rel_table):
    B, S = q.shape[:2]
    nb = S // BLOCK
    G = A_HEADS // A_KV_HEADS
    qb = q.reshape(B, nb, BLOCK, A_KV_HEADS, G, A_HEAD_DIM)

    def with_prev(t):
        tb = t.reshape(B, nb, BLOCK, A_KV_HEADS, A_HEAD_DIM)
        prev = jnp.pad(tb[:, :-1], ((0, 0), (1, 0), (0, 0), (0, 0), (0, 0)))
        return jnp.concatenate([prev, tb], axis=2)

    kb, vb = with_prev(k), with_prev(v)
    qi = jnp.arange(BLOCK)[:, None]
    kj = jnp.arange(2 * BLOCK)[None, :]
    dist = BLOCK + qi - kj
    in_win = (dist >= 0) & (dist < WINDOW)
    blk = jnp.arange(nb)[:, None, None]
    valid = in_win[None] & ((blk > 0) | (kj >= BLOCK)[None])

    bias = rel_table[t5_bucket(dist)]
    bias = bias.transpose(2, 0, 1).reshape(A_KV_HEADS, G, BLOCK, 2 * BLOCK).astype(jnp.float32)

    scale = 1.0 / math.sqrt(A_HEAD_DIM)
    s = jnp.einsum('bnqhgd,bnkhd->bnhgqk', qb, kb).astype(jnp.float32) * scale + bias
    s = jnp.where(valid[None, :, None, None], s, -jnp.inf)
    sink = sinks.astype(jnp.float32).reshape(A_KV_HEADS, G)[None, None, :, :, None, None]
    m = jnp.maximum(jnp.max(s, axis=-1, keepdims=True), sink)
    p = jnp.exp(s - m)
    denom = jnp.sum(p, axis=-1, keepdims=True) + jnp.exp(sink - m)
    p = (p / denom).astype(v.dtype)
    o = jnp.einsum('bnhgqk,bnkhd->bnqhgd', p, vb)
    return o.reshape(B, S, A_OUT)


def mla_attention(c_q, c_kv, k_rope, positions, q_norm, kv_norm, w_uq, w_ukv):
    B, S = c_q.shape[:2]
    q = (rmsnorm(c_q, q_norm) @ w_uq).reshape(B, S, B_HEADS, B_QK_DIM)
    q = jnp.concatenate([q[..., :NOPE_DIM], rope(q[..., NOPE_DIM:], positions)], axis=-1)
    kv = (rmsnorm(c_kv, kv_norm) @ w_ukv).reshape(B, S, B_HEADS, NOPE_DIM + V_DIM)
    k_nope, v = kv[..., :NOPE_DIM], kv[..., NOPE_DIM:]
    k_r = rope(k_rope, positions)
    k = jnp.concatenate([k_nope, jnp.broadcast_to(k_r[:, :, None, :], (B, S, B_HEADS, ROPE_DIM))], axis=-1)

    nb = S // BLOCK
    qb = q.reshape(B, nb, BLOCK, B_HEADS, B_QK_DIM).transpose(1, 0, 2, 3, 4)
    kpos = jnp.arange(S)
    scale = 1.0 / math.sqrt(B_QK_DIM)

    def attend(args):
        qblk, n = args
        s = jnp.einsum('bqhd,bkhd->bhqk', qblk, k).astype(jnp.float32) * scale
        qpos = n * BLOCK + jnp.arange(BLOCK)
        s = jnp.where(kpos[None, :] <= qpos[:, None], s, -jnp.inf)
        p = jax.nn.softmax(s, axis=-1).astype(v.dtype)
        return jnp.einsum('bhqk,bkhd->bqhd', p, v)

    o = lax.map(attend, (qb, jnp.arange(nb)))
    return o.transpose(1, 0, 2, 3, 4).reshape(B, S, B_OUT)


def setup_inputs(seed: int = 0) -> dict:
    key = jax.random.key(seed)
    ks = jax.random.split(key, 20)
    f32 = jnp.float32

    def nrm(k, shape, fan_in):
        return jax.random.normal(k, shape, f32) * (fan_in ** -0.5)

    def gain(k, shape):
        return 1.0 + 0.02 * jax.random.normal(k, shape, f32)

    x = jax.random.normal(ks[0], (BATCH, SEQ, D_MODEL), f32)
    offset = jax.random.randint(ks[1], (BATCH, 1), 0, 4096, dtype=jnp.int32)
    positions = (offset + jnp.arange(SEQ, dtype=jnp.int32)[None, :]).astype(jnp.int32)
    return {
        "x": x,
        "positions": positions,
        "rel_bias_table": 0.5 * jax.random.normal(ks[2], (N_BUCKETS, A_HEADS), f32),
        "norm_mix": gain(ks[3], (DEPTH, D_MODEL)),
        "w_in": nrm(ks[4], (DEPTH, D_MODEL, D_IN), D_MODEL),
        "attn_sinks": 0.5 * jax.random.normal(ks[5], (DEPTH, A_HEADS), f32),
        "q_norm": gain(ks[6], (DEPTH, Q_LORA)),
        "kv_norm": gain(ks[7], (DEPTH, KV_LORA)),
        "w_uq": nrm(ks[8], (DEPTH, Q_LORA, B_HEADS * B_QK_DIM), Q_LORA),
        "w_ukv": nrm(ks[9], (DEPTH, KV_LORA, B_HEADS * (NOPE_DIM + V_DIM)), KV_LORA),
        "w_branch_a": nrm(ks[10], (DEPTH, A_OUT, D_MODEL), A_OUT),
        "w_branch_b": nrm(ks[11], (DEPTH, B_OUT, D_MODEL), B_OUT),
        "w_out": nrm(ks[12], (DEPTH, D_MODEL, D_MODEL), D_MODEL),
        "norm_mlp": gain(ks[13], (DEPTH, D_MODEL)),
        "w_ff1": nrm(ks[14], (DEPTH, D_MODEL, D_FF), D_MODEL),
        "w_ff2": nrm(ks[15], (DEPTH, D_FF, D_MODEL), D_FF),
        "norm_final": gain(ks[16], (D_MODEL,)),
    }


def reference(x, positions, rel_bias_table, norm_mix, w_in, attn_sinks, q_norm, kv_norm,
              w_uq, w_ukv, w_branch_a, w_branch_b, w_out, norm_mlp, w_ff1, w_ff2, norm_final):
    split_at = [int(v) for v in np.cumsum(IN_SPLITS)[:-1]]
    for l in range(DEPTH):
        h = rmsnorm(x, norm_mix[l])
        z = h @ w_in[l]
        q_a, k_a, v_a, c_q, c_kv, k_rope, g_a, g_b = jnp.split(z, split_at, axis=-1)
        y_a = swa_sink_attention(q_a, k_a, v_a, attn_sinks[l], rel_bias_table) @ w_branch_a[l]
        y_b = mla_attention(c_q, c_kv, k_rope, positions, q_norm[l], kv_norm[l],
                            w_uq[l], w_ukv[l]) @ w_branch_b[l]
        merged = jax.nn.sigmoid(g_a) * y_a + jax.nn.sigmoid(g_b) * y_b
        x = x + merged @ w_out[l]
        h = rmsnorm(x, norm_mlp[l])
        x = x + jnp.square(jax.nn.relu(h @ w_ff1[l])) @ w_ff2[l]
    return rmsnorm(x, norm_final)
```

```python
import functools
import math

import jax
import jax.numpy as jnp
import numpy as np
from jax.experimental import pallas as pl
from jax.experimental.pallas import tpu as pltpu

D_MODEL = 1024
A_HEADS = 8
A_KV_HEADS = 2
A_HEAD_DIM = 64
WINDOW = 128
BLOCK = 128
B_HEADS = 8
Q_LORA = 384
KV_LORA = 256
NOPE_DIM = 64
ROPE_DIM = 32
V_DIM = 64
ROPE_THETA = 10000.0
N_BUCKETS = 32
MAX_DISTANCE = 128
D_FF = 4 * D_MODEL
EPS = 1e-5
A_Q_W = A_HEADS * A_HEAD_DIM
A_KV_W = A_KV_HEADS * A_HEAD_DIM
B_QK_DIM = NOPE_DIM + ROPE_DIM

LANES = 128
V7X_VMEM_LIMIT = 56 * 1024 * 1024

HEAD_PAD = LANES
ONES_ROWS = 16
TOK_TILE = 512
MLA_TILE = 512
SWA_TILE = 512

BF16 = jnp.bfloat16
F32 = jnp.float32


def _rms(x, g):
    return x * jax.lax.rsqrt(jnp.mean(x * x, axis=-1, keepdims=True) + EPS) * g


def _dot(a, b):
    return jnp.dot(a, b, preferred_element_type=F32)


def _in_proj_kernel(x_ref, pos_ref, invf_ref, g_ref, w1_ref, qn_ref, kvn_ref,
                    wuq_ref, wk_ref, wv_ref,
                    qat_ref, ka_ref, vat_ref, qbt_ref, kb_ref, vbt_ref, *, b_scale):
    tm = x_ref.shape[1]
    h = _rms(x_ref[0], g_ref[...]).astype(BF16)
    z = _dot(h, w1_ref[...])
    o = 0
    qa = z[:, o:o + A_Q_W]; o += A_Q_W
    ka = z[:, o:o + A_KV_W]; o += A_KV_W
    va = z[:, o:o + A_KV_W]; o += A_KV_W
    cq = z[:, o:o + Q_LORA]; o += Q_LORA
    ckv = z[:, o:o + KV_LORA]; o += KV_LORA
    kr = z[:, o:o + HEAD_PAD]

    qat_ref[0] = (qa * (1.0 / math.sqrt(A_HEAD_DIM))).T.astype(BF16)
    ka_ref[0] = ka.astype(BF16)
    vat_ref[0] = va.T.astype(BF16)

    cqn = _rms(cq, qn_ref[...]).astype(BF16)
    qb = _dot(cqn, wuq_ref[...]) * b_scale
    ckvn = _rms(ckv, kvn_ref[...]).astype(BF16)
    kb = _dot(ckvn, wk_ref[...])
    vb = _dot(ckvn, wv_ref[...])

    ang = pos_ref[0].astype(F32) * invf_ref[...]
    cos, sin = jnp.cos(ang), jnp.sin(ang)
    half = ROPE_DIM // 2
    r0, r1, r2 = NOPE_DIM, NOPE_DIM + half, NOPE_DIM + ROPE_DIM

    def rope_t(t):
        t1, t2 = t[r0:r1], t[r1:r2]
        return jnp.concatenate(
            [t[:r0], t1 * cos - t2 * sin, t2 * cos + t1 * sin, t[r2:]], axis=0)

    for hd in range(B_HEADS):
        qt = qb[:, hd * HEAD_PAD:(hd + 1) * HEAD_PAD].T
        qbt_ref[0, hd] = rope_t(qt).astype(BF16)

    kr_rot = rope_t(kr.T).T
    for hd in range(B_HEADS):
        kb_ref[0, hd] = (kb[:, hd * HEAD_PAD:(hd + 1) * HEAD_PAD] + kr_rot).astype(BF16)

    vbt = vb.T.astype(BF16)
    nsub = vbt_ref.shape[2]
    tk = vbt_ref.shape[4]
    for hd in range(B_HEADS):
        for s in range(nsub):
            vbt_ref[0, hd, s] = vbt[hd * V_DIM:(hd + 1) * V_DIM, s * tk:(s + 1) * tk]


def _in_proj(x, pos3, invf, g, w1, qn, kvn, wuq, wk, wv, *, tm, tk):
    B, S, _ = x.shape
    nsub = tm // tk
    grid = (B, S // tm)
    full = lambda shape: pl.BlockSpec(shape, lambda b, i: (0,) * len(shape))
    out_shape = (
        jax.ShapeDtypeStruct((B, A_Q_W, S), BF16),
        jax.ShapeDtypeStruct((B, S, A_KV_W), BF16),
        jax.ShapeDtypeStruct((B, A_KV_W, S), BF16),
        jax.ShapeDtypeStruct((B, B_HEADS, HEAD_PAD, S), BF16),
        jax.ShapeDtypeStruct((B, B_HEADS, S, HEAD_PAD), BF16),
        jax.ShapeDtypeStruct((B, B_HEADS, S // tk, V_DIM, tk), BF16),
    )
    out_specs = (
        pl.BlockSpec((1, A_Q_W, tm), lambda b, i: (b, 0, i)),
        pl.BlockSpec((1, tm, A_KV_W), lambda b, i: (b, i, 0)),
        pl.BlockSpec((1, A_KV_W, tm), lambda b, i: (b, 0, i)),
        pl.BlockSpec((1, B_HEADS, HEAD_PAD, tm), lambda b, i: (b, 0, 0, i)),
        pl.BlockSpec((1, B_HEADS, tm, HEAD_PAD), lambda b, i: (b, 0, i, 0)),
        pl.BlockSpec((1, B_HEADS, nsub, V_DIM, tk), lambda b, i: (b, 0, i, 0, 0)),
    )
    in_specs = [
        pl.BlockSpec((1, tm, D_MODEL), lambda b, i: (b, i, 0)),
        pl.BlockSpec((1, 1, tm), lambda b, i: (b, 0, i)),
        full(invf.shape), full(g.shape), full(w1.shape), full(qn.shape), full(kvn.shape),
        full(wuq.shape), full(wk.shape), full(wv.shape),
    ]
    return pl.pallas_call(
        functools.partial(_in_proj_kernel, b_scale=1.0 / math.sqrt(B_QK_DIM)),
        out_shape=out_shape, grid=grid, in_specs=in_specs, out_specs=out_specs,
        compiler_params=pltpu.CompilerParams(
            dimension_semantics=("parallel", "parallel"),
            vmem_limit_bytes=V7X_VMEM_LIMIT),
        name="in_proj",
    )(x, pos3, invf, g, w1, qn, kvn, wuq, wk, wv)


def _bias_kernel(table_ref, bkt_ref, out_ref):
    bkt = bkt_ref[...]
    row = jax.lax.broadcasted_iota(jnp.int32, bkt.shape, 0)
    for hd in range(A_HEADS):
        acc = jnp.full(bkt.shape, -jnp.inf, F32)
        for b in range(N_BUCKETS):
            acc = jnp.where(bkt == b, table_ref[b, hd], acc)
        out_ref[0, hd] = acc
        out_ref[1, hd] = jnp.where(row < BLOCK, -jnp.inf, acc)


def _bucket_table():
    kj = np.arange(2 * BLOCK)[:, None]
    qi = np.arange(BLOCK)[None, :]
    dist = BLOCK + qi - kj
    max_exact = N_BUCKETS // 2
    n = np.maximum(dist, 0)
    nf = np.maximum(n, 1).astype(np.float32)
    large = max_exact + (np.log(nf / max_exact) / math.log(MAX_DISTANCE / max_exact)
                         * (N_BUCKETS - max_exact)).astype(np.int32)
    large = np.minimum(large, N_BUCKETS - 1)
    bkt = np.where(n < max_exact, n, large)
    in_win = (dist >= 0) & (dist < WINDOW)
    return np.where(in_win, bkt, -1).astype(np.int32)


def _rel_bias(table):
    bkt = jnp.asarray(_bucket_table())
    return pl.pallas_call(
        _bias_kernel,
        out_shape=jax.ShapeDtypeStruct((2, A_HEADS, 2 * BLOCK, BLOCK), F32),
        in_specs=[pl.BlockSpec(memory_space=pltpu.SMEM),
                  pl.BlockSpec(memory_space=pltpu.VMEM)],
        out_specs=pl.BlockSpec(memory_space=pltpu.VMEM),
        name="rel_bias",
    )(table, bkt)


def _swa_kernel(sink_ref, qt_ref, kc_ref, kp_ref, vc_ref, vp_ref, bias_ref, o_ref):
    i = pl.program_id(1)
    ta = qt_ref.shape[2]
    group = A_HEADS // A_KV_HEADS
    kcat = jnp.concatenate([kp_ref[0], kc_ref[0]], axis=0)
    vcat = jnp.concatenate([vp_ref[0], vc_ref[0]], axis=1)
    ones = jnp.ones((ONES_ROWS, 2 * BLOCK), BF16)
    zpad = jnp.zeros((A_HEAD_DIM, BLOCK), BF16)
    for r in range(ta // BLOCK):
        kk = kcat[r * BLOCK:(r + 2) * BLOCK]
        vv = vcat[:, r * BLOCK:(r + 2) * BLOCK]
        variant = jnp.where(i == 0, 1, 0) if r == 0 else 0
        outs = []
        for hd in range(A_HEADS):
            g = hd // group
            qt = qt_ref[0, hd * A_HEAD_DIM:(hd + 1) * A_HEAD_DIM, r * BLOCK:(r + 1) * BLOCK]
            qpad = jnp.concatenate([qt, zpad] if g == 0 else [zpad, qt], axis=0)
            s = _dot(kk, qpad) + bias_ref[variant, hd]
            sink = sink_ref[hd]
            m = jnp.maximum(jnp.max(s, axis=0, keepdims=True), sink)
            p = jnp.exp(s - m).astype(BF16)
            vaug = jnp.concatenate([vv[g * A_HEAD_DIM:(g + 1) * A_HEAD_DIM], ones], axis=0)
            acc = _dot(vaug, p)
            denom = acc[A_HEAD_DIM:A_HEAD_DIM + 1] + jnp.exp(sink - m)
            outs.append(acc[:A_HEAD_DIM] / denom)
        ot = jnp.concatenate(outs, axis=0)
        o_ref[0, r * BLOCK:(r + 1) * BLOCK, :] = ot.T.astype(BF16)


def _swa(sinks, qat, ka, vat, bias, *, ta):
    B, _, S = qat.shape
    sub = ta // BLOCK
    prev = lambda i: jnp.maximum(i * sub - 1, 0)
    return pl.pallas_call(
        _swa_kernel,
        out_shape=jax.ShapeDtypeStruct((B, S, A_Q_W), BF16),
        grid=(B, S // ta),
        in_specs=[
            pl.BlockSpec(memory_space=pltpu.SMEM),
            pl.BlockSpec((1, A_Q_W, ta), lambda b, i: (b, 0, i)),
            pl.BlockSpec((1, ta, A_KV_W), lambda b, i: (b, i, 0)),
            pl.BlockSpec((1, BLOCK, A_KV_W), lambda b, i: (b, prev(i), 0)),
            pl.BlockSpec((1, A_KV_W, ta), lambda b, i: (b, 0, i)),
            pl.BlockSpec((1, A_KV_W, BLOCK), lambda b, i: (b, 0, prev(i))),
            pl.BlockSpec(bias.shape, lambda b, i: (0, 0, 0, 0)),
        ],
        out_specs=pl.BlockSpec((1, ta, A_Q_W), lambda b, i: (b, i, 0)),
        compiler_params=pltpu.CompilerParams(
            dimension_semantics=("parallel", "parallel"),
            vmem_limit_bytes=V7X_VMEM_LIMIT),
        name="swa",
    )(sinks, qat, ka, ka, vat, vat, bias)


def _mla_kernel(qt_ref, k_ref, vt_ref, o_ref):
    i = pl.program_id(2)
    t = qt_ref.shape[3]
    nh = qt_ref.shape[1]
    ones = jnp.ones((ONES_ROWS, t), BF16)
    qts = [qt_ref[0, hh] for hh in range(nh)]

    def step(j, carry, masked):
        new = []
        for hh in range(nh):
            m, acc = carry[2 * hh], carry[2 * hh + 1]
            kblk = k_ref[0, hh, pl.ds(pl.multiple_of(j * t, t), t), :]
            s = _dot(kblk, qts[hh])
            if masked:
                krow = jax.lax.broadcasted_iota(jnp.int32, s.shape, 0)
                qcol = jax.lax.broadcasted_iota(jnp.int32, s.shape, 1)
                s = jnp.where(krow <= qcol, s, -jnp.inf)
            m_new = jnp.maximum(m, jnp.max(s, axis=0, keepdims=True))
            p = jnp.exp(s - m_new).astype(BF16)
            alpha = jnp.exp(m - m_new)
            vaug = jnp.concatenate([vt_ref[0, hh, j], ones], axis=0)
            acc = alpha * acc + _dot(vaug, p)
            new += [m_new, acc]
        return tuple(new)

    init = []
    for _ in range(nh):
        init += [jnp.full((1, t), -jnp.inf, F32), jnp.zeros((V_DIM + ONES_ROWS, t), F32)]
    carry = jax.lax.fori_loop(0, i, lambda j, c: step(j, c, False), tuple(init))
    carry = step(i, carry, True)
    outs = []
    for hh in range(nh):
        acc = carry[2 * hh + 1]
        outs.append(acc[:V_DIM] / acc[V_DIM:V_DIM + 1])
    o_ref[0] = jnp.concatenate(outs, axis=0).T.astype(BF16)


def _mla(qbt, kb, vbt, *, t, nh=2):
    B, H, _, S = qbt.shape
    return pl.pallas_call(
        _mla_kernel,
        out_shape=jax.ShapeDtypeStruct((B, S, H * V_DIM), BF16),
        grid=(B, H // nh, S // t),
        in_specs=[
            pl.BlockSpec((1, nh, HEAD_PAD, t), lambda b, hp, i: (b, hp, 0, i)),
            pl.BlockSpec((1, nh, S, HEAD_PAD), lambda b, hp, i: (b, hp, 0, 0)),
            pl.BlockSpec((1, nh, S // t, V_DIM, t), lambda b, hp, i: (b, hp, 0, 0, 0)),
        ],
        out_specs=pl.BlockSpec((1, t, nh * V_DIM), lambda b, hp, i: (b, i, hp)),
        compiler_params=pltpu.CompilerParams(
            dimension_semantics=("parallel", "parallel", "arbitrary"),
            vmem_limit_bytes=V7X_VMEM_LIMIT),
        name="mla",
    )(qbt, kb, vbt)


def _merge_kernel(x_ref, oa_ref, ob_ref, g_ref, wg_ref, wa_ref, wb_ref, wo_ref, o_ref):
    x = x_ref[0]
    h = _rms(x, g_ref[...]).astype(BF16)
    gates = jax.nn.sigmoid(_dot(h, wg_ref[...]))
    ya = _dot(oa_ref[0], wa_ref[...])
    yb = _dot(ob_ref[0], wb_ref[...])
    merged = gates[:, :D_MODEL] * ya + gates[:, D_MODEL:] * yb
    o_ref[0] = x + _dot(merged.astype(BF16), wo_ref[...])


def _merge(x, oa, ob, g, wg, wa, wb, wo, *, tm):
    B, S, _ = x.shape
    full = lambda shape: pl.BlockSpec(shape, lambda b, i: (0,) * len(shape))
    tok = lambda w: pl.BlockSpec((1, tm, w), lambda b, i: (b, i, 0))
    return pl.pallas_call(
        _merge_kernel,
        out_shape=jax.ShapeDtypeStruct(x.shape, F32),
        grid=(B, S // tm),
        in_specs=[tok(D_MODEL), tok(A_Q_W), tok(B_HEADS * V_DIM), full(g.shape),
                  full(wg.shape), full(wa.shape), full(wb.shape), full(wo.shape)],
        out_specs=tok(D_MODEL),
        compiler_params=pltpu.CompilerParams(
            dimension_semantics=("parallel", "parallel"),
            vmem_limit_bytes=V7X_VMEM_LIMIT),
        name="merge",
    )(x, oa, ob, g, wg, wa, wb, wo)


def _mlp_kernel(x_ref, g_ref, w1_ref, w2_ref, gf_ref, o_ref, *, final):
    x = x_ref[0]
    h = _rms(x, g_ref[...]).astype(BF16)
    u = jnp.square(jnp.maximum(_dot(h, w1_ref[...]), 0.0)).astype(BF16)
    y = x + _dot(u, w2_ref[...])
    if final:
        y = _rms(y, gf_ref[...])
    o_ref[0] = y


def _mlp(x, g, w1, w2, gf, *, tm, final):
    B, S, _ = x.shape
    full = lambda shape: pl.BlockSpec(shape, lambda b, i: (0,) * len(shape))
    tok = pl.BlockSpec((1, tm, D_MODEL), lambda b, i: (b, i, 0))
    return pl.pallas_call(
        functools.partial(_mlp_kernel, final=final),
        out_shape=jax.ShapeDtypeStruct(x.shape, F32),
        grid=(B, S // tm),
        in_specs=[tok, full(g.shape), full(w1.shape), full(w2.shape), full(gf.shape)],
        out_specs=tok,
        compiler_params=pltpu.CompilerParams(
            dimension_semantics=("parallel", "parallel"),
            vmem_limit_bytes=V7X_VMEM_LIMIT),
        name="mlp_final" if final else "mlp",
    )(x, g, w1, w2, gf)


def _prep_layer(w_in, w_uq, w_ukv):
    o = 0
    wqa = w_in[:, o:o + A_Q_W]; o += A_Q_W
    wka = w_in[:, o:o + A_KV_W]; o += A_KV_W
    wva = w_in[:, o:o + A_KV_W]; o += A_KV_W
    wcq = w_in[:, o:o + Q_LORA]; o += Q_LORA
    wckv = w_in[:, o:o + KV_LORA]; o += KV_LORA
    wkr = w_in[:, o:o + ROPE_DIM]; o += ROPE_DIM
    wg = w_in[:, o:]
    wkr_pad = jnp.pad(wkr, ((0, 0), (NOPE_DIM, HEAD_PAD - NOPE_DIM - ROPE_DIM)))
    w1 = jnp.concatenate([wqa, wka, wva, wcq, wckv, wkr_pad], axis=1).astype(BF16)
    wuq = w_uq.reshape(Q_LORA, B_HEADS, B_QK_DIM)
    wuq = jnp.pad(wuq, ((0, 0), (0, 0), (0, HEAD_PAD - B_QK_DIM)))
    wuq = wuq.reshape(Q_LORA, B_HEADS * HEAD_PAD).astype(BF16)
    wukv = w_ukv.reshape(KV_LORA, B_HEADS, NOPE_DIM + V_DIM)
    wk = jnp.pad(wukv[:, :, :NOPE_DIM], ((0, 0), (0, 0), (0, HEAD_PAD - NOPE_DIM)))
    wk = wk.reshape(KV_LORA, B_HEADS * HEAD_PAD).astype(BF16)
    wv = wukv[:, :, NOPE_DIM:].reshape(KV_LORA, B_HEADS * V_DIM).astype(BF16)
    return w1, wg.astype(BF16), wuq, wk, wv


def kernel(x, positions, rel_bias_table, norm_mix, w_in, attn_sinks, q_norm, kv_norm,
           w_uq, w_ukv, w_branch_a, w_branch_b, w_out, norm_mlp, w_ff1, w_ff2, norm_final):
    B, S, _ = x.shape
    depth = w_in.shape[0]
    tm = min(TOK_TILE, S)
    t = min(MLA_TILE, S)
    ta = min(SWA_TILE, S)
    assert S % tm == 0 and S % t == 0 and S % ta == 0 and tm % t == 0 and ta % BLOCK == 0

    half = ROPE_DIM // 2
    inv_freq = ROPE_THETA ** (-jnp.arange(half, dtype=F32) / half)
    invf = inv_freq.reshape(half, 1)
    pos3 = positions.reshape(B, 1, S)
    bias = _rel_bias(rel_bias_table.astype(F32))
    row = lambda v: v.reshape(1, -1).astype(F32)

    for l in range(depth):
        w1, wg, wuq, wk, wv = _prep_layer(w_in[l], w_uq[l], w_ukv[l])
        qat, ka, vat, qbt, kb, vbt = _in_proj(
            x, pos3, invf, row(norm_mix[l]), w1, row(q_norm[l]), row(kv_norm[l]),
            wuq, wk, wv, tm=tm, tk=t)
        oa = _swa(attn_sinks[l].astype(F32), qat, ka, vat, bias, ta=ta)
        ob = _mla(qbt, kb, vbt, t=t)
        x = _merge(x, oa, ob, row(norm_mix[l]), wg, w_branch_a[l].astype(BF16),
                   w_branch_b[l].astype(BF16), w_out[l].astype(BF16), tm=tm)
        x = _mlp(x, row(norm_mlp[l]), w_ff1[l].astype(BF16), w_ff2[l].astype(BF16),
                 row(norm_final), tm=tm, final=(l == depth - 1))
    return x
```

```python
import functools
import math

import jax
import jax.numpy as jnp
import numpy as np
from jax.experimental import pallas as pl
from jax.experimental.pallas import tpu as pltpu

D_MODEL = 1024
A_HEADS = 8
A_KV_HEADS = 2
A_HEAD_DIM = 64
WINDOW = 128
BLOCK = 128
B_HEADS = 8
Q_LORA = 384
KV_LORA = 256
NOPE_DIM = 64
ROPE_DIM = 32
V_DIM = 64
ROPE_THETA = 10000.0
N_BUCKETS = 32
MAX_DISTANCE = 128
D_FF = 4 * D_MODEL
EPS = 1e-5
A_Q_W = A_HEADS * A_HEAD_DIM
A_KV_W = A_KV_HEADS * A_HEAD_DIM
B_QK_DIM = NOPE_DIM + ROPE_DIM

LANES = 128
V7X_VMEM_LIMIT = 56 * 1024 * 1024

HEAD_PAD = LANES
ONES_ROWS = 16
TOK_TILE = 512
MLA_TILE = 512
MLA_KEYS = 256
SWA_TILE = 512

BF16 = jnp.bfloat16
F32 = jnp.float32


def _rms(x, g):
    return x * jax.lax.rsqrt(jnp.mean(x * x, axis=-1, keepdims=True) + EPS) * g


def _dot(a, b):
    return jnp.dot(a, b, preferred_element_type=F32)


def _in_proj_kernel(x_ref, pos_ref, invf_ref, g_ref, w1_ref, qn_ref, kvn_ref,
                    wuq_ref, wk_ref, wv_ref,
                    qat_ref, ka_ref, vat_ref, qbt_ref, kb_ref, vbt_ref, *, b_scale):
    tm = x_ref.shape[1]
    h = _rms(x_ref[0], g_ref[...]).astype(BF16)
    z = _dot(h, w1_ref[...])
    o = 0
    qa = z[:, o:o + A_Q_W]; o += A_Q_W
    ka = z[:, o:o + A_KV_W]; o += A_KV_W
    va = z[:, o:o + A_KV_W]; o += A_KV_W
    cq = z[:, o:o + Q_LORA]; o += Q_LORA
    ckv = z[:, o:o + KV_LORA]; o += KV_LORA
    kr = z[:, o:o + HEAD_PAD]

    qat_ref[0] = (qa * (1.0 / math.sqrt(A_HEAD_DIM))).T.astype(BF16)
    ka_ref[0] = ka.astype(BF16)
    vat_ref[0] = va.T.astype(BF16)

    cqn = _rms(cq, qn_ref[...]).astype(BF16)
    qb = _dot(cqn, wuq_ref[...]) * b_scale
    ckvn = _rms(ckv, kvn_ref[...]).astype(BF16)
    kb = _dot(ckvn, wk_ref[...])
    vb = _dot(ckvn, wv_ref[...])

    ang = pos_ref[0].astype(F32) * invf_ref[...]
    cos, sin = jnp.cos(ang), jnp.sin(ang)
    half = ROPE_DIM // 2
    r0, r1, r2 = NOPE_DIM, NOPE_DIM + half, NOPE_DIM + ROPE_DIM

    def rope_t(t):
        t1, t2 = t[r0:r1], t[r1:r2]
        return jnp.concatenate(
            [t[:r0], t1 * cos - t2 * sin, t2 * cos + t1 * sin, t[r2:]], axis=0)

    for hd in range(B_HEADS):
        qt = qb[:, hd * HEAD_PAD:(hd + 1) * HEAD_PAD].T
        qbt_ref[0, hd] = rope_t(qt).astype(BF16)

    kr_rot = rope_t(kr.T).T
    for hd in range(B_HEADS):
        kb_ref[0, hd] = (kb[:, hd * HEAD_PAD:(hd + 1) * HEAD_PAD] + kr_rot).astype(BF16)

    vbt = vb.T.astype(BF16)
    nsub = vbt_ref.shape[2]
    tk = vbt_ref.shape[4]
    for hd in range(B_HEADS):
        for s in range(nsub):
            vbt_ref[0, hd, s] = vbt[hd * V_DIM:(hd + 1) * V_DIM, s * tk:(s + 1) * tk]


def _in_proj(x, pos3, invf, g, w1, qn, kvn, wuq, wk, wv, *, tm, tk):
    B, S, _ = x.shape
    nsub = tm // tk
    grid = (B, S // tm)
    full = lambda shape: pl.BlockSpec(shape, lambda b, i: (0,) * len(shape))
    out_shape = (
        jax.ShapeDtypeStruct((B, A_Q_W, S), BF16),
        jax.ShapeDtypeStruct((B, S, A_KV_W), BF16),
        jax.ShapeDtypeStruct((B, A_KV_W, S), BF16),
        jax.ShapeDtypeStruct((B, B_HEADS, HEAD_PAD, S), BF16),
        jax.ShapeDtypeStruct((B, B_HEADS, S, HEAD_PAD), BF16),
        jax.ShapeDtypeStruct((B, B_HEADS, S // tk, V_DIM, tk), BF16),
    )
    out_specs = (
        pl.BlockSpec((1, A_Q_W, tm), lambda b, i: (b, 0, i)),
        pl.BlockSpec((1, tm, A_KV_W), lambda b, i: (b, i, 0)),
        pl.BlockSpec((1, A_KV_W, tm), lambda b, i: (b, 0, i)),
        pl.BlockSpec((1, B_HEADS, HEAD_PAD, tm), lambda b, i: (b, 0, 0, i)),
        pl.BlockSpec((1, B_HEADS, tm, HEAD_PAD), lambda b, i: (b, 0, i, 0)),
        pl.BlockSpec((1, B_HEADS, nsub, V_DIM, tk), lambda b, i: (b, 0, i, 0, 0)),
    )
    in_specs = [
        pl.BlockSpec((1, tm, D_MODEL), lambda b, i: (b, i, 0)),
        pl.BlockSpec((1, 1, tm), lambda b, i: (b, 0, i)),
        full(invf.shape), full(g.shape), full(w1.shape), full(qn.shape), full(kvn.shape),
        full(wuq.shape), full(wk.shape), full(wv.shape),
    ]
    return pl.pallas_call(
        functools.partial(_in_proj_kernel, b_scale=math.log2(math.e) / math.sqrt(B_QK_DIM)),
        out_shape=out_shape, grid=grid, in_specs=in_specs, out_specs=out_specs,
        compiler_params=pltpu.CompilerParams(
            dimension_semantics=("parallel", "parallel"),
            vmem_limit_bytes=V7X_VMEM_LIMIT),
        name="in_proj",
    )(x, pos3, invf, g, w1, qn, kvn, wuq, wk, wv)


def _bias_kernel(table_ref, bkt_ref, out_ref):
    bkt = bkt_ref[...]
    row = jax.lax.broadcasted_iota(jnp.int32, bkt.shape, 0)
    for hd in range(A_HEADS):
        acc = jnp.full(bkt.shape, -jnp.inf, F32)
        for b in range(N_BUCKETS):
            acc = jnp.where(bkt == b, table_ref[b, hd], acc)
        out_ref[0, hd] = acc
        out_ref[1, hd] = jnp.where(row < BLOCK, -jnp.inf, acc)


def _bucket_table():
    kj = np.arange(2 * BLOCK)[:, None]
    qi = np.arange(BLOCK)[None, :]
    dist = BLOCK + qi - kj
    max_exact = N_BUCKETS // 2
    n = np.maximum(dist, 0)
    nf = np.maximum(n, 1).astype(np.float32)
    large = max_exact + (np.log(nf / max_exact) / math.log(MAX_DISTANCE / max_exact)
                         * (N_BUCKETS - max_exact)).astype(np.int32)
    large = np.minimum(large, N_BUCKETS - 1)
    bkt = np.where(n < max_exact, n, large)
    in_win = (dist >= 0) & (dist < WINDOW)
    return np.where(in_win, bkt, -1).astype(np.int32)


def _rel_bias(table):
    bkt = jnp.asarray(_bucket_table())
    return pl.pallas_call(
        _bias_kernel,
        out_shape=jax.ShapeDtypeStruct((2, A_HEADS, 2 * BLOCK, BLOCK), F32),
        in_specs=[pl.BlockSpec(memory_space=pltpu.SMEM),
                  pl.BlockSpec(memory_space=pltpu.VMEM)],
        out_specs=pl.BlockSpec(memory_space=pltpu.VMEM),
        name="rel_bias",
    )(table, bkt)


def _swa_kernel(sink_ref, qt_ref, kc_ref, kp_ref, vc_ref, vp_ref, bias_ref, o_ref):
    i = pl.program_id(1)
    ta = qt_ref.shape[2]
    group = A_HEADS // A_KV_HEADS
    kcat = jnp.concatenate([kp_ref[0], kc_ref[0]], axis=0)
    vcat = jnp.concatenate([vp_ref[0], vc_ref[0]], axis=1)
    ones = jnp.ones((ONES_ROWS, 2 * BLOCK), BF16)
    zpad = jnp.zeros((A_HEAD_DIM, BLOCK), BF16)
    for r in range(ta // BLOCK):
        kk = kcat[r * BLOCK:(r + 2) * BLOCK]
        vv = vcat[:, r * BLOCK:(r + 2) * BLOCK]
        variant = jnp.where(i == 0, 1, 0) if r == 0 else 0
        outs = []
        for hd in range(A_HEADS):
            g = hd // group
            qt = qt_ref[0, hd * A_HEAD_DIM:(hd + 1) * A_HEAD_DIM, r * BLOCK:(r + 1) * BLOCK]
            qpad = jnp.concatenate([qt, zpad] if g == 0 else [zpad, qt], axis=0)
            s = _dot(kk, qpad) + bias_ref[variant, hd]
            sink = sink_ref[hd]
            m = jnp.maximum(jnp.max(s, axis=0, keepdims=True), sink)
            p = jnp.exp(s - m).astype(BF16)
            vaug = jnp.concatenate([vv[g * A_HEAD_DIM:(g + 1) * A_HEAD_DIM], ones], axis=0)
            acc = _dot(vaug, p)
            denom = acc[A_HEAD_DIM:A_HEAD_DIM + 1] + jnp.exp(sink - m)
            outs.append(acc[:A_HEAD_DIM] / denom)
        ot = jnp.concatenate(outs, axis=0)
        o_ref[0, r * BLOCK:(r + 1) * BLOCK, :] = ot.T.astype(BF16)


def _swa(sinks, qat, ka, vat, bias, *, ta):
    B, _, S = qat.shape
    sub = ta // BLOCK
    prev = lambda i: jnp.maximum(i * sub - 1, 0)
    return pl.pallas_call(
        _swa_kernel,
        out_shape=jax.ShapeDtypeStruct((B, S, A_Q_W), BF16),
        grid=(B, S // ta),
        in_specs=[
            pl.BlockSpec(memory_space=pltpu.SMEM),
            pl.BlockSpec((1, A_Q_W, ta), lambda b, i: (b, 0, i)),
            pl.BlockSpec((1, ta, A_KV_W), lambda b, i: (b, i, 0)),
            pl.BlockSpec((1, BLOCK, A_KV_W), lambda b, i: (b, prev(i), 0)),
            pl.BlockSpec((1, A_KV_W, ta), lambda b, i: (b, 0, i)),
            pl.BlockSpec((1, A_KV_W, BLOCK), lambda b, i: (b, 0, prev(i))),
            pl.BlockSpec(bias.shape, lambda b, i: (0, 0, 0, 0)),
        ],
        out_specs=pl.BlockSpec((1, ta, A_Q_W), lambda b, i: (b, i, 0)),
        compiler_params=pltpu.CompilerParams(
            dimension_semantics=("parallel", "parallel"),
            vmem_limit_bytes=V7X_VMEM_LIMIT),
        name="swa",
    )(sinks, qat, ka, ka, vat, vat, bias)


def _mla_kernel(qt_ref, k_ref, vt_ref, o_ref, *scratch):
    i = pl.program_id(2)
    tq = qt_ref.shape[3]
    tk = vt_ref.shape[4]
    nh = qt_ref.shape[1]
    assert tq == 2 * tk
    s_refs = [scratch[2 * hh:2 * hh + 2] for hh in range(nh)]
    p_refs = [scratch[2 * nh + 2 * hh:2 * nh + 2 * hh + 2] for hh in range(nh)]
    ones = jnp.ones((ONES_ROWS, tk), BF16)

    def qk(blk, par):
        for hh in range(nh):
            kblk = k_ref[0, hh, pl.ds(pl.multiple_of(blk * tk, tk), tk), :]
            s_refs[hh][par][...] = _dot(kblk, qt_ref[0, hh])

    def softmax(par, ms, diag=None):
        new_ms, alphas = [], []
        for hh in range(nh):
            s = s_refs[hh][par][...]
            if diag is not None:
                krow = jax.lax.broadcasted_iota(jnp.int32, s.shape, 0) + (2 * i + diag) * tk
                qcol = jax.lax.broadcasted_iota(jnp.int32, s.shape, 1) + i * tq
                s = jnp.where(krow <= qcol, s, -jnp.inf)
            m_new = jnp.maximum(ms[hh], jnp.max(s, axis=0, keepdims=True))
            p_refs[hh][par][...] = jnp.exp2(s - m_new).astype(BF16)
            alphas.append(jnp.exp2(ms[hh] - m_new))
            new_ms.append(m_new)
        return new_ms, alphas

    def pv(blk, par, alphas, accs):
        out = []
        for hh in range(nh):
            vaug = jnp.concatenate([vt_ref[0, hh, blk], ones], axis=0)
            out.append(alphas[hh] * accs[hh] + _dot(vaug, p_refs[hh][par][...]))
        return out

    def pair(jj, carry):
        ms, alphas, accs = carry
        a = 2 * jj
        ms, alphas_a = softmax(0, ms)
        accs = pv(jnp.maximum(a - 1, 0), 1, alphas, accs)
        qk(a + 1, 1)
        ms, alphas_b = softmax(1, ms)
        accs = pv(a, 0, alphas_a, accs)
        qk(a + 2, 0)
        return ms, alphas_b, accs

    for hh in range(nh):
        p_refs[hh][1][...] = jnp.zeros((tk, tq), BF16)
    qk(0, 0)
    ms = [jnp.full((1, tq), -jnp.inf, F32) for _ in range(nh)]
    alphas = [jnp.ones((1, tq), F32) for _ in range(nh)]
    accs = [jnp.zeros((V_DIM + ONES_ROWS, tq), F32) for _ in range(nh)]
    ms, alphas, accs = jax.lax.fori_loop(0, i, pair, (ms, alphas, accs))

    d0 = 2 * i
    qk(d0 + 1, 1)
    ms, alphas_a = softmax(0, ms, diag=0)
    accs = pv(jnp.maximum(d0 - 1, 0), 1, alphas, accs)
    ms, alphas_b = softmax(1, ms, diag=1)
    accs = pv(d0, 0, alphas_a, accs)
    accs = pv(d0 + 1, 1, alphas_b, accs)
    outs = [acc[:V_DIM] / acc[V_DIM:V_DIM + 1] for acc in accs]
    o_ref[0] = jnp.concatenate(outs, axis=0).T.astype(BF16)


def _mla(qbt, kb, vbt, *, tq, nh=2):
    B, H, _, S = qbt.shape
    tk = vbt.shape[4]
    return pl.pallas_call(
        _mla_kernel,
        out_shape=jax.ShapeDtypeStruct((B, S, H * V_DIM), BF16),
        grid=(B, H // nh, S // tq),
        in_specs=[
            pl.BlockSpec((1, nh, HEAD_PAD, tq), lambda b, hp, i: (b, hp, 0, i)),
            pl.BlockSpec((1, nh, S, HEAD_PAD), lambda b, hp, i: (b, hp, 0, 0)),
            pl.BlockSpec((1, nh, S // tk, V_DIM, tk), lambda b, hp, i: (b, hp, 0, 0, 0)),
        ],
        out_specs=pl.BlockSpec((1, tq, nh * V_DIM), lambda b, hp, i: (b, i, hp)),
        scratch_shapes=([pltpu.VMEM((tk, tq), F32)] * (2 * nh)
                        + [pltpu.VMEM((tk, tq), BF16)] * (2 * nh)),
        compiler_params=pltpu.CompilerParams(
            dimension_semantics=("parallel", "parallel", "arbitrary"),
            vmem_limit_bytes=V7X_VMEM_LIMIT),
        name="mla",
    )(qbt, kb, vbt)


def _merge_kernel(x_ref, oa_ref, ob_ref, g_ref, wg_ref, wa_ref, wb_ref, wo_ref, o_ref):
    x = x_ref[0]
    h = _rms(x, g_ref[...]).astype(BF16)
    gates = jax.nn.sigmoid(_dot(h, wg_ref[...]))
    ya = _dot(oa_ref[0], wa_ref[...])
    yb = _dot(ob_ref[0], wb_ref[...])
    merged = gates[:, :D_MODEL] * ya + gates[:, D_MODEL:] * yb
    o_ref[0] = x + _dot(merged.astype(BF16), wo_ref[...])


def _merge(x, oa, ob, g, wg, wa, wb, wo, *, tm):
    B, S, _ = x.shape
    full = lambda shape: pl.BlockSpec(shape, lambda b, i: (0,) * len(shape))
    tok = lambda w: pl.BlockSpec((1, tm, w), lambda b, i: (b, i, 0))
    return pl.pallas_call(
        _merge_kernel,
        out_shape=jax.ShapeDtypeStruct(x.shape, F32),
        grid=(B, S // tm),
        in_specs=[tok(D_MODEL), tok(A_Q_W), tok(B_HEADS * V_DIM), full(g.shape),
                  full(wg.shape), full(wa.shape), full(wb.shape), full(wo.shape)],
        out_specs=tok(D_MODEL),
        compiler_params=pltpu.CompilerParams(
            dimension_semantics=("parallel", "parallel"),
            vmem_limit_bytes=V7X_VMEM_LIMIT),
        name="merge",
    )(x, oa, ob, g, wg, wa, wb, wo)


def _mlp_kernel(x_ref, g_ref, w1_ref, w2_ref, gf_ref, o_ref, *, final):
    x = x_ref[0]
    h = _rms(x, g_ref[...]).astype(BF16)
    u = jnp.square(jnp.maximum(_dot(h, w1_ref[...]), 0.0)).astype(BF16)
    y = x + _dot(u, w2_ref[...])
    if final:
        y = _rms(y, gf_ref[...])
    o_ref[0] = y


def _mlp(x, g, w1, w2, gf, *, tm, final):
    B, S, _ = x.shape
    full = lambda shape: pl.BlockSpec(shape, lambda b, i: (0,) * len(shape))
    tok = pl.BlockSpec((1, tm, D_MODEL), lambda b, i: (b, i, 0))
    return pl.pallas_call(
        functools.partial(_mlp_kernel, final=final),
        out_shape=jax.ShapeDtypeStruct(x.shape, F32),
        grid=(B, S // tm),
        in_specs=[tok, full(g.shape), full(w1.shape), full(w2.shape), full(gf.shape)],
        out_specs=tok,
        compiler_params=pltpu.CompilerParams(
            dimension_semantics=("parallel", "parallel"),
            vmem_limit_bytes=V7X_VMEM_LIMIT),
        name="mlp_final" if final else "mlp",
    )(x, g, w1, w2, gf)


def _prep_layer(w_in, w_uq, w_ukv):
    o = 0
    wqa = w_in[:, o:o + A_Q_W]; o += A_Q_W
    wka = w_in[:, o:o + A_KV_W]; o += A_KV_W
    wva = w_in[:, o:o + A_KV_W]; o += A_KV_W
    wcq = w_in[:, o:o + Q_LORA]; o += Q_LORA
    wckv = w_in[:, o:o + KV_LORA]; o += KV_LORA
    wkr = w_in[:, o:o + ROPE_DIM]; o += ROPE_DIM
    wg = w_in[:, o:]
    wkr_pad = jnp.pad(wkr, ((0, 0), (NOPE_DIM, HEAD_PAD - NOPE_DIM - ROPE_DIM)))
    w1 = jnp.concatenate([wqa, wka, wva, wcq, wckv, wkr_pad], axis=1).astype(BF16)
    wuq = w_uq.reshape(Q_LORA, B_HEADS, B_QK_DIM)
    wuq = jnp.pad(wuq, ((0, 0), (0, 0), (0, HEAD_PAD - B_QK_DIM)))
    wuq = wuq.reshape(Q_LORA, B_HEADS * HEAD_PAD).astype(BF16)
    wukv = w_ukv.reshape(KV_LORA, B_HEADS, NOPE_DIM + V_DIM)
    wk = jnp.pad(wukv[:, :, :NOPE_DIM], ((0, 0), (0, 0), (0, HEAD_PAD - NOPE_DIM)))
    wk = wk.reshape(KV_LORA, B_HEADS * HEAD_PAD).astype(BF16)
    wv = wukv[:, :, NOPE_DIM:].reshape(KV_LORA, B_HEADS * V_DIM).astype(BF16)
    return w1, wg.astype(BF16), wuq, wk, wv


def kernel(x, positions, rel_bias_table, norm_mix, w_in, attn_sinks, q_norm, kv_norm,
           w_uq, w_ukv, w_branch_a, w_branch_b, w_out, norm_mlp, w_ff1, w_ff2, norm_final):
    B, S, _ = x.shape
    depth = w_in.shape[0]
    tm = min(TOK_TILE, S)
    t = min(MLA_TILE, S)
    tk = MLA_KEYS
    ta = min(SWA_TILE, S)
    assert S % tm == 0 and S % t == 0 and S % ta == 0 and tm % tk == 0 and t == 2 * tk and ta % BLOCK == 0

    half = ROPE_DIM // 2
    inv_freq = ROPE_THETA ** (-jnp.arange(half, dtype=F32) / half)
    invf = inv_freq.reshape(half, 1)
    pos3 = positions.reshape(B, 1, S)
    bias = _rel_bias(rel_bias_table.astype(F32))
    row = lambda v: v.reshape(1, -1).astype(F32)

    for l in range(depth):
        w1, wg, wuq, wk, wv = _prep_layer(w_in[l], w_uq[l], w_ukv[l])
        qat, ka, vat, qbt, kb, vbt = _in_proj(
            x, pos3, invf, row(norm_mix[l]), w1, row(q_norm[l]), row(kv_norm[l]),
            wuq, wk, wv, tm=tm, tk=tk)
        oa = _swa(attn_sinks[l].astype(F32), qat, ka, vat, bias, ta=ta)
        ob = _mla(qbt, kb, vbt, tq=t)
        x = _merge(x, oa, ob, row(norm_mix[l]), wg, w_branch_a[l].astype(BF16),
                   w_branch_b[l].astype(BF16), w_out[l].astype(BF16), tm=tm)
        x = _mlp(x, row(norm_mlp[l]), w_ff1[l].astype(BF16), w_ff2[l].astype(BF16),
                 row(norm_final), tm=tm, final=(l == depth - 1))
    return x
```

```python
import functools
import math

import jax
import jax.numpy as jnp
import numpy as np
from jax.experimental import pallas as pl
from jax.experimental.pallas import tpu as pltpu

D_MODEL = 1024
A_HEADS = 8
A_KV_HEADS = 2
A_HEAD_DIM = 64
WINDOW = 128
BLOCK = 128
B_HEADS = 8
Q_LORA = 384
KV_LORA = 256
NOPE_DIM = 64
ROPE_DIM = 32
V_DIM = 64
ROPE_THETA = 10000.0
N_BUCKETS = 32
MAX_DISTANCE = 128
D_FF = 4 * D_MODEL
EPS = 1e-5
A_Q_W = A_HEADS * A_HEAD_DIM
A_KV_W = A_KV_HEADS * A_HEAD_DIM
B_QK_DIM = NOPE_DIM + ROPE_DIM

LANES = 128
V7X_VMEM_LIMIT = 56 * 1024 * 1024

HEAD_PAD = LANES
ONES_ROWS = 16
TOK_TILE = 512
MLA_TILE = 512
MLA_KEYS = 256
SWA_TILE = 512

BF16 = jnp.bfloat16
F32 = jnp.float32
LOG2E = math.log2(math.e)


def _rms(x, g):
    return x * jax.lax.rsqrt(jnp.mean(x * x, axis=-1, keepdims=True) + EPS) * g


def _dot(a, b):
    return jnp.dot(a, b, preferred_element_type=F32)


def _in_proj_kernel(x_ref, pos_ref, invf_ref, g_ref, w1_ref, qn_ref, kvn_ref,
                    wuq_ref, wk_ref, wv_ref,
                    qat_ref, ka_ref, vat_ref, qbt_ref, kb_ref, vbt_ref, *, b_scale):
    tm = x_ref.shape[1]
    h = _rms(x_ref[0], g_ref[...]).astype(BF16)
    z = _dot(h, w1_ref[...])
    o = 0
    qa = z[:, o:o + A_Q_W]; o += A_Q_W
    ka = z[:, o:o + A_KV_W]; o += A_KV_W
    va = z[:, o:o + A_KV_W]; o += A_KV_W
    cq = z[:, o:o + Q_LORA]; o += Q_LORA
    ckv = z[:, o:o + KV_LORA]; o += KV_LORA
    kr = z[:, o:o + HEAD_PAD]

    qat_ref[0] = (qa * (LOG2E / math.sqrt(A_HEAD_DIM))).T.astype(BF16)
    ka_ref[0] = ka.astype(BF16)
    vat_ref[0] = va.T.astype(BF16)

    cqn = _rms(cq, qn_ref[...]).astype(BF16)
    qb = _dot(cqn, wuq_ref[...]) * b_scale
    ckvn = _rms(ckv, kvn_ref[...]).astype(BF16)
    kb = _dot(ckvn, wk_ref[...])
    vb = _dot(ckvn, wv_ref[...])

    ang = pos_ref[0].astype(F32) * invf_ref[...]
    cos, sin = jnp.cos(ang), jnp.sin(ang)
    half = ROPE_DIM // 2
    r0, r1, r2 = NOPE_DIM, NOPE_DIM + half, NOPE_DIM + ROPE_DIM

    def rope_t(t):
        t1, t2 = t[r0:r1], t[r1:r2]
        return jnp.concatenate(
            [t[:r0], t1 * cos - t2 * sin, t2 * cos + t1 * sin, t[r2:]], axis=0)

    for hd in range(B_HEADS):
        qt = qb[:, hd * HEAD_PAD:(hd + 1) * HEAD_PAD].T
        qbt_ref[0, hd] = rope_t(qt).astype(BF16)

    kr_rot = rope_t(kr.T).T
    for hd in range(B_HEADS):
        kb_ref[0, hd] = (kb[:, hd * HEAD_PAD:(hd + 1) * HEAD_PAD] + kr_rot).astype(BF16)

    vbt = vb.T.astype(BF16)
    nsub = vbt_ref.shape[2]
    tk = vbt_ref.shape[4]
    for hd in range(B_HEADS):
        for s in range(nsub):
            vbt_ref[0, hd, s] = vbt[hd * V_DIM:(hd + 1) * V_DIM, s * tk:(s + 1) * tk]


def _in_proj(x, pos3, invf, g, w1, qn, kvn, wuq, wk, wv, *, tm, tk):
    B, S, _ = x.shape
    nsub = tm // tk
    grid = (B, S // tm)
    full = lambda shape: pl.BlockSpec(shape, lambda b, i: (0,) * len(shape))
    out_shape = (
        jax.ShapeDtypeStruct((B, A_Q_W, S), BF16),
        jax.ShapeDtypeStruct((B, S, A_KV_W), BF16),
        jax.ShapeDtypeStruct((B, A_KV_W, S), BF16),
        jax.ShapeDtypeStruct((B, B_HEADS, HEAD_PAD, S), BF16),
        jax.ShapeDtypeStruct((B, B_HEADS, S, HEAD_PAD), BF16),
        jax.ShapeDtypeStruct((B, B_HEADS, S // tk, V_DIM, tk), BF16),
    )
    out_specs = (
        pl.BlockSpec((1, A_Q_W, tm), lambda b, i: (b, 0, i)),
        pl.BlockSpec((1, tm, A_KV_W), lambda b, i: (b, i, 0)),
        pl.BlockSpec((1, A_KV_W, tm), lambda b, i: (b, 0, i)),
        pl.BlockSpec((1, B_HEADS, HEAD_PAD, tm), lambda b, i: (b, 0, 0, i)),
        pl.BlockSpec((1, B_HEADS, tm, HEAD_PAD), lambda b, i: (b, 0, i, 0)),
        pl.BlockSpec((1, B_HEADS, nsub, V_DIM, tk), lambda b, i: (b, 0, i, 0, 0)),
    )
    in_specs = [
        pl.BlockSpec((1, tm, D_MODEL), lambda b, i: (b, i, 0)),
        pl.BlockSpec((1, 1, tm), lambda b, i: (b, 0, i)),
        full(invf.shape), full(g.shape), full(w1.shape), full(qn.shape), full(kvn.shape),
        full(wuq.shape), full(wk.shape), full(wv.shape),
    ]
    return pl.pallas_call(
        functools.partial(_in_proj_kernel, b_scale=LOG2E / math.sqrt(B_QK_DIM)),
        out_shape=out_shape, grid=grid, in_specs=in_specs, out_specs=out_specs,
        compiler_params=pltpu.CompilerParams(
            dimension_semantics=("parallel", "parallel"),
            vmem_limit_bytes=V7X_VMEM_LIMIT),
        name="in_proj",
    )(x, pos3, invf, g, w1, qn, kvn, wuq, wk, wv)


def _bias_kernel(table_ref, bkt_ref, out_ref):
    bkt = bkt_ref[...]
    row = jax.lax.broadcasted_iota(jnp.int32, bkt.shape, 0)
    group = A_HEADS // A_KV_HEADS
    for hd in range(A_HEADS):
        acc = jnp.full(bkt.shape, -jnp.inf, F32)
        for b in range(N_BUCKETS):
            acc = jnp.where(bkt == b, table_ref[b, hd] * LOG2E, acc)
        g, lane = hd // group, (hd % group) * BLOCK
        out_ref[0, g, :, lane:lane + BLOCK] = acc
        out_ref[1, g, :, lane:lane + BLOCK] = jnp.where(row < BLOCK, -jnp.inf, acc)


def _bucket_table():
    kj = np.arange(2 * BLOCK)[:, None]
    qi = np.arange(BLOCK)[None, :]
    dist = BLOCK + qi - kj
    max_exact = N_BUCKETS // 2
    n = np.maximum(dist, 0)
    nf = np.maximum(n, 1).astype(np.float32)
    large = max_exact + (np.log(nf / max_exact) / math.log(MAX_DISTANCE / max_exact)
                         * (N_BUCKETS - max_exact)).astype(np.int32)
    large = np.minimum(large, N_BUCKETS - 1)
    bkt = np.where(n < max_exact, n, large)
    in_win = (dist >= 0) & (dist < WINDOW)
    return np.where(in_win, bkt, -1).astype(np.int32)


def _rel_bias(table):
    bkt = jnp.asarray(_bucket_table())
    return pl.pallas_call(
        _bias_kernel,
        out_shape=jax.ShapeDtypeStruct(
            (2, A_KV_HEADS, 2 * BLOCK, (A_HEADS // A_KV_HEADS) * BLOCK), F32),
        in_specs=[pl.BlockSpec(memory_space=pltpu.SMEM),
                  pl.BlockSpec(memory_space=pltpu.VMEM)],
        out_specs=pl.BlockSpec(memory_space=pltpu.VMEM),
        name="rel_bias",
    )(table, bkt)


def _swa_kernel(sink_ref, qt_ref, kc_ref, kp_ref, vc_ref, vp_ref, bias_ref, o_ref):
    i = pl.program_id(1)
    ta = qt_ref.shape[2]
    group = A_HEADS // A_KV_HEADS
    width = group * BLOCK
    kcat = jnp.concatenate([kp_ref[0], kc_ref[0]], axis=0)
    vcat = jnp.concatenate([vp_ref[0], vc_ref[0]], axis=1)
    ones = jnp.ones((ONES_ROWS, 2 * BLOCK), BF16)
    zpad = jnp.zeros((A_HEAD_DIM, width), BF16)
    units = [(r, g) for r in range(ta // BLOCK) for g in range(A_KV_HEADS)]

    def scores(r, g):
        kk = kcat[r * BLOCK:(r + 2) * BLOCK]
        qrow = jnp.concatenate(
            [qt_ref[0, hd * A_HEAD_DIM:(hd + 1) * A_HEAD_DIM, r * BLOCK:(r + 1) * BLOCK]
             for hd in range(g * group, (g + 1) * group)], axis=1)
        qpad = jnp.concatenate([qrow, zpad] if g == 0 else [zpad, qrow], axis=0)
        variant = jnp.where(i == 0, 1, 0) if r == 0 else 0
        return _dot(kk, qpad) + bias_ref[variant, g]

    def attend(r, g, s):
        sink = sink_ref[g]
        m = jnp.maximum(jnp.max(s, axis=0, keepdims=True), sink)
        p = jnp.exp2(s - m).astype(BF16)
        vv = vcat[g * A_HEAD_DIM:(g + 1) * A_HEAD_DIM, r * BLOCK:(r + 2) * BLOCK]
        acc = _dot(jnp.concatenate([vv, ones], axis=0), p)
        denom = acc[A_HEAD_DIM:A_HEAD_DIM + 1] + jnp.exp2(sink - m)
        return acc[:A_HEAD_DIM] / denom

    outs = {}
    s_next = scores(*units[0])
    for n, (r, g) in enumerate(units):
        s = s_next
        if n + 1 < len(units):
            s_next = scores(*units[n + 1])
        outs[(r, g)] = attend(r, g, s)
    for r in range(ta // BLOCK):
        heads = [outs[(r, g)][:, hh * BLOCK:(hh + 1) * BLOCK]
                 for g in range(A_KV_HEADS) for hh in range(group)]
        o_ref[0, r * BLOCK:(r + 1) * BLOCK, :] = jnp.concatenate(heads, axis=0).T.astype(BF16)


def _swa(sink_rows, qat, ka, vat, bias, *, ta):
    B, _, S = qat.shape
    sub = ta // BLOCK
    prev = lambda i: jnp.maximum(i * sub - 1, 0)
    return pl.pallas_call(
        _swa_kernel,
        out_shape=jax.ShapeDtypeStruct((B, S, A_Q_W), BF16),
        grid=(B, S // ta),
        in_specs=[
            pl.BlockSpec(sink_rows.shape, lambda b, i: (0, 0, 0)),
            pl.BlockSpec((1, A_Q_W, ta), lambda b, i: (b, 0, i)),
            pl.BlockSpec((1, ta, A_KV_W), lambda b, i: (b, i, 0)),
            pl.BlockSpec((1, BLOCK, A_KV_W), lambda b, i: (b, prev(i), 0)),
            pl.BlockSpec((1, A_KV_W, ta), lambda b, i: (b, 0, i)),
            pl.BlockSpec((1, A_KV_W, BLOCK), lambda b, i: (b, 0, prev(i))),
            pl.BlockSpec(bias.shape, lambda b, i: (0, 0, 0, 0)),
        ],
        out_specs=pl.BlockSpec((1, ta, A_Q_W), lambda b, i: (b, i, 0)),
        compiler_params=pltpu.CompilerParams(
            dimension_semantics=("parallel", "parallel"),
            vmem_limit_bytes=V7X_VMEM_LIMIT),
        name="swa",
    )(sink_rows, qat, ka, ka, vat, vat, bias)


def _mla_kernel(qt_ref, k_ref, vt_ref, o_ref, *scratch):
    i = pl.program_id(2)
    tq = qt_ref.shape[3]
    tk = vt_ref.shape[4]
    nh = qt_ref.shape[1]
    assert tq == 2 * tk
    s_refs = [scratch[2 * hh:2 * hh + 2] for hh in range(nh)]
    p_refs = [scratch[2 * nh + 2 * hh:2 * nh + 2 * hh + 2] for hh in range(nh)]
    ones = jnp.ones((ONES_ROWS, tk), BF16)

    def qk(hh, blk, par):
        kblk = k_ref[0, hh, pl.ds(pl.multiple_of(blk * tk, tk), tk), :]
        s = _dot(kblk, qt_ref[0, hh])
        s_refs[hh][par][...] = s
        return jnp.max(s, axis=0, keepdims=True)

    def softmax(hh, par, m, cmax, diag=None):
        s = s_refs[hh][par][...]
        if diag is not None:
            krow = jax.lax.broadcasted_iota(jnp.int32, s.shape, 0) + (2 * i + diag) * tk
            qcol = jax.lax.broadcasted_iota(jnp.int32, s.shape, 1) + i * tq
            s = jnp.where(krow <= qcol, s, -jnp.inf)
            cmax = jnp.max(s, axis=0, keepdims=True)
        m_new = jnp.maximum(m, cmax)
        p_refs[hh][par][...] = jnp.exp2(s - m_new).astype(BF16)
        return m_new, jnp.exp2(m - m_new)

    def pv(hh, blk, par, alpha, acc):
        vaug = jnp.concatenate([vt_ref[0, hh, blk], ones], axis=0)
        return alpha * acc + _dot(vaug, p_refs[hh][par][...])

    heads = range(nh)

    def pair(jj, carry):
        ms, alphas, accs, cm0 = carry
        a = 2 * jj
        prev = jnp.maximum(a - 1, 0)
        ms, alphas_a = zip(*[softmax(hh, 0, ms[hh], cm0[hh]) for hh in heads])
        accs = [pv(hh, prev, 1, alphas[hh], accs[hh]) for hh in heads]
        cm1 = [qk(hh, a + 1, 1) for hh in heads]
        ms, alphas_b = zip(*[softmax(hh, 1, ms[hh], cm1[hh]) for hh in heads])
        accs = [pv(hh, a, 0, alphas_a[hh], accs[hh]) for hh in heads]
        cm0 = [qk(hh, a + 2, 0) for hh in heads]
        return list(ms), list(alphas_b), accs, cm0

    for hh in heads:
        p_refs[hh][1][...] = jnp.zeros((tk, tq), BF16)
    cm0 = [qk(hh, 0, 0) for hh in heads]
    ms = [jnp.full((1, tq), -jnp.inf, F32) for _ in heads]
    alphas = [jnp.ones((1, tq), F32) for _ in heads]
    accs = [jnp.zeros((V_DIM + ONES_ROWS, tq), F32) for _ in heads]
    ms, alphas, accs, _ = jax.lax.fori_loop(0, i, pair, (ms, alphas, accs, cm0))

    d0 = 2 * i
    for hh in heads:
        qk(hh, d0 + 1, 1)
    ms, alphas_a = zip(*[softmax(hh, 0, ms[hh], None, diag=0) for hh in heads])
    accs = [pv(hh, jnp.maximum(d0 - 1, 0), 1, alphas[hh], accs[hh]) for hh in heads]
    ms, alphas_b = zip(*[softmax(hh, 1, ms[hh], None, diag=1) for hh in heads])
    accs = [pv(hh, d0, 0, alphas_a[hh], accs[hh]) for hh in heads]
    accs = [pv(hh, d0 + 1, 1, alphas_b[hh], accs[hh]) for hh in heads]
    outs = [acc[:V_DIM] / acc[V_DIM:V_DIM + 1] for acc in accs]
    o_ref[0] = jnp.concatenate(outs, axis=0).T.astype(BF16)


def _mla(qbt, kb, vbt, *, tq, nh=2):
    B, H, _, S = qbt.shape
    tk = vbt.shape[4]
    return pl.pallas_call(
        _mla_kernel,
        out_shape=jax.ShapeDtypeStruct((B, S, H * V_DIM), BF16),
        grid=(B, H // nh, S // tq),
        in_specs=[
            pl.BlockSpec((1, nh, HEAD_PAD, tq), lambda b, hp, i: (b, hp, 0, i)),
            pl.BlockSpec((1, nh, S, HEAD_PAD), lambda b, hp, i: (b, hp, 0, 0)),
            pl.BlockSpec((1, nh, S // tk, V_DIM, tk), lambda b, hp, i: (b, hp, 0, 0, 0)),
        ],
        out_specs=pl.BlockSpec((1, tq, nh * V_DIM), lambda b, hp, i: (b, i, hp)),
        scratch_shapes=([pltpu.VMEM((tk, tq), F32)] * (2 * nh)
                        + [pltpu.VMEM((tk, tq), BF16)] * (2 * nh)),
        compiler_params=pltpu.CompilerParams(
            dimension_semantics=("parallel", "parallel", "arbitrary"),
            vmem_limit_bytes=V7X_VMEM_LIMIT),
        name="mla",
    )(qbt, kb, vbt)


def _merge_kernel(x_ref, oa_ref, ob_ref, g_ref, wg_ref, wa_ref, wb_ref, wo_ref, o_ref):
    x = x_ref[0]
    h = _rms(x, g_ref[...]).astype(BF16)
    gates = jax.nn.sigmoid(_dot(h, wg_ref[...]))
    ya = _dot(oa_ref[0], wa_ref[...])
    yb = _dot(ob_ref[0], wb_ref[...])
    merged = gates[:, :D_MODEL] * ya + gates[:, D_MODEL:] * yb
    o_ref[0] = x + _dot(merged.astype(BF16), wo_ref[...])


def _merge(x, oa, ob, g, wg, wa, wb, wo, *, tm):
    B, S, _ = x.shape
    full = lambda shape: pl.BlockSpec(shape, lambda b, i: (0,) * len(shape))
    tok = lambda w: pl.BlockSpec((1, tm, w), lambda b, i: (b, i, 0))
    return pl.pallas_call(
        _merge_kernel,
        out_shape=jax.ShapeDtypeStruct(x.shape, F32),
        grid=(B, S // tm),
        in_specs=[tok(D_MODEL), tok(A_Q_W), tok(B_HEADS * V_DIM), full(g.shape),
                  full(wg.shape), full(wa.shape), full(wb.shape), full(wo.shape)],
        out_specs=tok(D_MODEL),
        compiler_params=pltpu.CompilerParams(
            dimension_semantics=("parallel", "parallel"),
            vmem_limit_bytes=V7X_VMEM_LIMIT),
        name="merge",
    )(x, oa, ob, g, wg, wa, wb, wo)


def _mlp_kernel(x_ref, g_ref, w1_ref, w2_ref, gf_ref, o_ref, *, final):
    x = x_ref[0]
    h = _rms(x, g_ref[...]).astype(BF16)
    u = jnp.square(jnp.maximum(_dot(h, w1_ref[...]), 0.0)).astype(BF16)
    y = x + _dot(u, w2_ref[...])
    if final:
        y = _rms(y, gf_ref[...])
    o_ref[0] = y


def _mlp(x, g, w1, w2, gf, *, tm, final):
    B, S, _ = x.shape
    full = lambda shape: pl.BlockSpec(shape, lambda b, i: (0,) * len(shape))
    tok = pl.BlockSpec((1, tm, D_MODEL), lambda b, i: (b, i, 0))
    return pl.pallas_call(
        functools.partial(_mlp_kernel, final=final),
        out_shape=jax.ShapeDtypeStruct(x.shape, F32),
        grid=(B, S // tm),
        in_specs=[tok, full(g.shape), full(w1.shape), full(w2.shape), full(gf.shape)],
        out_specs=tok,
        compiler_params=pltpu.CompilerParams(
            dimension_semantics=("parallel", "parallel"),
            vmem_limit_bytes=V7X_VMEM_LIMIT),
        name="mlp_final" if final else "mlp",
    )(x, g, w1, w2, gf)


def _prep_layer(w_in, w_uq, w_ukv):
    o = 0
    wqa = w_in[:, o:o + A_Q_W]; o += A_Q_W
    wka = w_in[:, o:o + A_KV_W]; o += A_KV_W
    wva = w_in[:, o:o + A_KV_W]; o += A_KV_W
    wcq = w_in[:, o:o + Q_LORA]; o += Q_LORA
    wckv = w_in[:, o:o + KV_LORA]; o += KV_LORA
    wkr = w_in[:, o:o + ROPE_DIM]; o += ROPE_DIM
    wg = w_in[:, o:]
    wkr_pad = jnp.pad(wkr, ((0, 0), (NOPE_DIM, HEAD_PAD - NOPE_DIM - ROPE_DIM)))
    w1 = jnp.concatenate([wqa, wka, wva, wcq, wckv, wkr_pad], axis=1).astype(BF16)
    wuq = w_uq.reshape(Q_LORA, B_HEADS, B_QK_DIM)
    wuq = jnp.pad(wuq, ((0, 0), (0, 0), (0, HEAD_PAD - B_QK_DIM)))
    wuq = wuq.reshape(Q_LORA, B_HEADS * HEAD_PAD).astype(BF16)
    wukv = w_ukv.reshape(KV_LORA, B_HEADS, NOPE_DIM + V_DIM)
    wk = jnp.pad(wukv[:, :, :NOPE_DIM], ((0, 0), (0, 0), (0, HEAD_PAD - NOPE_DIM)))
    wk = wk.reshape(KV_LORA, B_HEADS * HEAD_PAD).astype(BF16)
    wv = wukv[:, :, NOPE_DIM:].reshape(KV_LORA, B_HEADS * V_DIM).astype(BF16)
    return w1, wg.astype(BF16), wuq, wk, wv


def kernel(x, positions, rel_bias_table, norm_mix, w_in, attn_sinks, q_norm, kv_norm,
           w_uq, w_ukv, w_branch_a, w_branch_b, w_out, norm_mlp, w_ff1, w_ff2, norm_final):
    B, S, _ = x.shape
    depth = w_in.shape[0]
    tm = min(TOK_TILE, S)
    t = min(MLA_TILE, S)
    tk = MLA_KEYS
    ta = min(SWA_TILE, S)
    assert S % tm == 0 and S % t == 0 and S % ta == 0 and tm % tk == 0 and t == 2 * tk and ta % BLOCK == 0

    half = ROPE_DIM // 2
    inv_freq = ROPE_THETA ** (-jnp.arange(half, dtype=F32) / half)
    invf = inv_freq.reshape(half, 1)
    pos3 = positions.reshape(B, 1, S)
    bias = _rel_bias(rel_bias_table.astype(F32))
    row = lambda v: v.reshape(1, -1).astype(F32)

    for l in range(depth):
        w1, wg, wuq, wk, wv = _prep_layer(w_in[l], w_uq[l], w_ukv[l])
        qat, ka, vat, qbt, kb, vbt = _in_proj(
            x, pos3, invf, row(norm_mix[l]), w1, row(q_norm[l]), row(kv_norm[l]),
            wuq, wk, wv, tm=tm, tk=tk)
        sink_rows = jnp.repeat(attn_sinks[l].astype(F32).reshape(A_KV_HEADS, 1, -1) * LOG2E,
                               BLOCK, axis=2)
        oa = _swa(sink_rows, qat, ka, vat, bias, ta=ta)
        ob = _mla(qbt, kb, vbt, tq=t)
        x = _merge(x, oa, ob, row(norm_mix[l]), wg, w_branch_a[l].astype(BF16),
                   w_branch_b[l].astype(BF16), w_out[l].astype(BF16), tm=tm)
        x = _mlp(x, row(norm_mlp[l]), w_ff1[l].astype(BF16), w_ff2[l].astype(BF16),
                 row(norm_final), tm=tm, final=(l == depth - 1))
    return x
```

```python
import functools
import math

import jax
import jax.numpy as jnp
import numpy as np
from jax.experimental import pallas as pl
from jax.experimental.pallas import tpu as pltpu

D_MODEL = 1024
A_HEADS = 8
A_KV_HEADS = 2
A_HEAD_DIM = 64
WINDOW = 128
BLOCK = 128
B_HEADS = 8
Q_LORA = 384
KV_LORA = 256
NOPE_DIM = 64
ROPE_DIM = 32
V_DIM = 64
ROPE_THETA = 10000.0
N_BUCKETS = 32
MAX_DISTANCE = 128
D_FF = 4 * D_MODEL
EPS = 1e-5
A_Q_W = A_HEADS * A_HEAD_DIM
A_KV_W = A_KV_HEADS * A_HEAD_DIM
B_QK_DIM = NOPE_DIM + ROPE_DIM

LANES = 128
V7X_VMEM_LIMIT = 56 * 1024 * 1024

HEAD_PAD = LANES
ONES_ROWS = 16
TOK_TILE = 512
MLA_TILE = 512
MLA_KEYS = 256
SWA_TILE = 512

BF16 = jnp.bfloat16
F32 = jnp.float32
LOG2E = math.log2(math.e)


def _rms(x, g):
    return x * jax.lax.rsqrt(jnp.mean(x * x, axis=-1, keepdims=True) + EPS) * g


def _dot(a, b):
    return jnp.dot(a, b, preferred_element_type=F32)


def _in_proj_kernel(x_ref, pos_ref, invf_ref, g_ref, w1_ref, qn_ref, kvn_ref,
                    wuq_ref, wk_ref, wv_ref,
                    qat_ref, ka_ref, vat_ref, qbt_ref, kb_ref, vbt_ref, *, b_scale):
    tm = x_ref.shape[1]
    h = _rms(x_ref[0], g_ref[...]).astype(BF16)
    z = _dot(h, w1_ref[...])
    o = 0
    qa = z[:, o:o + A_Q_W]; o += A_Q_W
    ka = z[:, o:o + A_KV_W]; o += A_KV_W
    va = z[:, o:o + A_KV_W]; o += A_KV_W
    cq = z[:, o:o + Q_LORA]; o += Q_LORA
    ckv = z[:, o:o + KV_LORA]; o += KV_LORA
    kr = z[:, o:o + HEAD_PAD]

    qat_ref[0] = (qa * (LOG2E / math.sqrt(A_HEAD_DIM))).T.astype(BF16)
    ka_ref[0] = ka.astype(BF16)
    vat_ref[0] = va.T.astype(BF16)

    cqn = _rms(cq, qn_ref[...]).astype(BF16)
    qb = _dot(cqn, wuq_ref[...]) * b_scale
    ckvn = _rms(ckv, kvn_ref[...]).astype(BF16)
    kb = _dot(ckvn, wk_ref[...])
    vb = _dot(ckvn, wv_ref[...])

    ang = pos_ref[0].astype(F32) * invf_ref[...]
    cos, sin = jnp.cos(ang), jnp.sin(ang)
    half = ROPE_DIM // 2
    r0, r1, r2 = NOPE_DIM, NOPE_DIM + half, NOPE_DIM + ROPE_DIM

    def rope_t(t):
        t1, t2 = t[r0:r1], t[r1:r2]
        return jnp.concatenate(
            [t[:r0], t1 * cos - t2 * sin, t2 * cos + t1 * sin, t[r2:]], axis=0)

    for hd in range(B_HEADS):
        qt = qb[:, hd * HEAD_PAD:(hd + 1) * HEAD_PAD].T
        qbt_ref[0, hd] = rope_t(qt).astype(BF16)

    kr_rot = rope_t(kr.T).T
    for hd in range(B_HEADS):
        kb_ref[0, hd] = (kb[:, hd * HEAD_PAD:(hd + 1) * HEAD_PAD] + kr_rot).astype(BF16)

    vbt = vb.T.astype(BF16)
    nsub = vbt_ref.shape[2]
    tk = vbt_ref.shape[4]
    for hd in range(B_HEADS):
        for s in range(nsub):
            vbt_ref[0, hd, s] = vbt[hd * V_DIM:(hd + 1) * V_DIM, s * tk:(s + 1) * tk]


def _in_proj(x, pos3, invf, g, w1, qn, kvn, wuq, wk, wv, *, tm, tk):
    B, S, _ = x.shape
    nsub = tm // tk
    grid = (B, S // tm)
    full = lambda shape: pl.BlockSpec(shape, lambda b, i: (0,) * len(shape))
    out_shape = (
        jax.ShapeDtypeStruct((B, A_Q_W, S), BF16),
        jax.ShapeDtypeStruct((B, S, A_KV_W), BF16),
        jax.ShapeDtypeStruct((B, A_KV_W, S), BF16),
        jax.ShapeDtypeStruct((B, B_HEADS, HEAD_PAD, S), BF16),
        jax.ShapeDtypeStruct((B, B_HEADS, S, HEAD_PAD), BF16),
        jax.ShapeDtypeStruct((B, B_HEADS, S // tk, V_DIM, tk), BF16),
    )
    out_specs = (
        pl.BlockSpec((1, A_Q_W, tm), lambda b, i: (b, 0, i)),
        pl.BlockSpec((1, tm, A_KV_W), lambda b, i: (b, i, 0)),
        pl.BlockSpec((1, A_KV_W, tm), lambda b, i: (b, 0, i)),
        pl.BlockSpec((1, B_HEADS, HEAD_PAD, tm), lambda b, i: (b, 0, 0, i)),
        pl.BlockSpec((1, B_HEADS, tm, HEAD_PAD), lambda b, i: (b, 0, i, 0)),
        pl.BlockSpec((1, B_HEADS, nsub, V_DIM, tk), lambda b, i: (b, 0, i, 0, 0)),
    )
    in_specs = [
        pl.BlockSpec((1, tm, D_MODEL), lambda b, i: (b, i, 0)),
        pl.BlockSpec((1, 1, tm), lambda b, i: (b, 0, i)),
        full(invf.shape), full(g.shape), full(w1.shape), full(qn.shape), full(kvn.shape),
        full(wuq.shape), full(wk.shape), full(wv.shape),
    ]
    return pl.pallas_call(
        functools.partial(_in_proj_kernel, b_scale=LOG2E / math.sqrt(B_QK_DIM)),
        out_shape=out_shape, grid=grid, in_specs=in_specs, out_specs=out_specs,
        compiler_params=pltpu.CompilerParams(
            dimension_semantics=("parallel", "parallel"),
            vmem_limit_bytes=V7X_VMEM_LIMIT),
        name="in_proj",
    )(x, pos3, invf, g, w1, qn, kvn, wuq, wk, wv)


def _bias_kernel(table_ref, bkt_ref, out_ref):
    bkt = bkt_ref[...]
    row = jax.lax.broadcasted_iota(jnp.int32, bkt.shape, 0)
    group = A_HEADS // A_KV_HEADS
    for hd in range(A_HEADS):
        acc = jnp.full(bkt.shape, -jnp.inf, F32)
        for b in range(N_BUCKETS):
            acc = jnp.where(bkt == b, table_ref[b, hd] * LOG2E, acc)
        g, lane = hd // group, (hd % group) * BLOCK
        out_ref[0, g, :, lane:lane + BLOCK] = acc
        out_ref[1, g, :, lane:lane + BLOCK] = jnp.where(row < BLOCK, -jnp.inf, acc)


def _bucket_table():
    kj = np.arange(2 * BLOCK)[:, None]
    qi = np.arange(BLOCK)[None, :]
    dist = BLOCK + qi - kj
    max_exact = N_BUCKETS // 2
    n = np.maximum(dist, 0)
    nf = np.maximum(n, 1).astype(np.float32)
    large = max_exact + (np.log(nf / max_exact) / math.log(MAX_DISTANCE / max_exact)
                         * (N_BUCKETS - max_exact)).astype(np.int32)
    large = np.minimum(large, N_BUCKETS - 1)
    bkt = np.where(n < max_exact, n, large)
    in_win = (dist >= 0) & (dist < WINDOW)
    return np.where(in_win, bkt, -1).astype(np.int32)


def _rel_bias(table):
    bkt = jnp.asarray(_bucket_table())
    return pl.pallas_call(
        _bias_kernel,
        out_shape=jax.ShapeDtypeStruct(
            (2, A_KV_HEADS, 2 * BLOCK, (A_HEADS // A_KV_HEADS) * BLOCK), F32),
        in_specs=[pl.BlockSpec(memory_space=pltpu.SMEM),
                  pl.BlockSpec(memory_space=pltpu.VMEM)],
        out_specs=pl.BlockSpec(memory_space=pltpu.VMEM),
        name="rel_bias",
    )(table, bkt)


def _swa_kernel(sink_ref, qt_ref, kc_ref, kp_ref, vc_ref, vp_ref, bias_ref, o_ref):
    i = pl.program_id(1)
    ta = qt_ref.shape[2]
    group = A_HEADS // A_KV_HEADS
    width = group * BLOCK
    kcat = jnp.concatenate([kp_ref[0], kc_ref[0]], axis=0)
    vcat = jnp.concatenate([vp_ref[0], vc_ref[0]], axis=1)
    ones = jnp.ones((ONES_ROWS, 2 * BLOCK), BF16)
    zpad = jnp.zeros((A_HEAD_DIM, width), BF16)
    units = [(r, g) for r in range(ta // BLOCK) for g in range(A_KV_HEADS)]

    def scores(r, g):
        kk = kcat[r * BLOCK:(r + 2) * BLOCK]
        qrow = jnp.concatenate(
            [qt_ref[0, hd * A_HEAD_DIM:(hd + 1) * A_HEAD_DIM, r * BLOCK:(r + 1) * BLOCK]
             for hd in range(g * group, (g + 1) * group)], axis=1)
        qpad = jnp.concatenate([qrow, zpad] if g == 0 else [zpad, qrow], axis=0)
        variant = jnp.where(i == 0, 1, 0) if r == 0 else 0
        return _dot(kk, qpad) + bias_ref[variant, g]

    def attend(r, g, s):
        sink = sink_ref[g]
        m = jnp.maximum(jnp.max(s, axis=0, keepdims=True), sink)
        p = jnp.exp2(s - m).astype(BF16)
        vv = vcat[g * A_HEAD_DIM:(g + 1) * A_HEAD_DIM, r * BLOCK:(r + 2) * BLOCK]
        acc = _dot(jnp.concatenate([vv, ones], axis=0), p)
        denom = acc[A_HEAD_DIM:A_HEAD_DIM + 1] + jnp.exp2(sink - m)
        return acc[:A_HEAD_DIM] / denom

    outs = {}
    s_next = scores(*units[0])
    for n, (r, g) in enumerate(units):
        s = s_next
        if n + 1 < len(units):
            s_next = scores(*units[n + 1])
        outs[(r, g)] = attend(r, g, s)
    for r in range(ta // BLOCK):
        heads = [outs[(r, g)][:, hh * BLOCK:(hh + 1) * BLOCK]
                 for g in range(A_KV_HEADS) for hh in range(group)]
        o_ref[0, r * BLOCK:(r + 1) * BLOCK, :] = jnp.concatenate(heads, axis=0).T.astype(BF16)


def _swa(sink_rows, qat, ka, vat, bias, *, ta):
    B, _, S = qat.shape
    sub = ta // BLOCK
    prev = lambda i: jnp.maximum(i * sub - 1, 0)
    return pl.pallas_call(
        _swa_kernel,
        out_shape=jax.ShapeDtypeStruct((B, S, A_Q_W), BF16),
        grid=(B, S // ta),
        in_specs=[
            pl.BlockSpec(sink_rows.shape, lambda b, i: (0, 0, 0)),
            pl.BlockSpec((1, A_Q_W, ta), lambda b, i: (b, 0, i)),
            pl.BlockSpec((1, ta, A_KV_W), lambda b, i: (b, i, 0)),
            pl.BlockSpec((1, BLOCK, A_KV_W), lambda b, i: (b, prev(i), 0)),
            pl.BlockSpec((1, A_KV_W, ta), lambda b, i: (b, 0, i)),
            pl.BlockSpec((1, A_KV_W, BLOCK), lambda b, i: (b, 0, prev(i))),
            pl.BlockSpec(bias.shape, lambda b, i: (0, 0, 0, 0)),
        ],
        out_specs=pl.BlockSpec((1, ta, A_Q_W), lambda b, i: (b, i, 0)),
        compiler_params=pltpu.CompilerParams(
            dimension_semantics=("parallel", "parallel"),
            vmem_limit_bytes=V7X_VMEM_LIMIT),
        name="swa",
    )(sink_rows, qat, ka, ka, vat, vat, bias)


def _mla_kernel(qt_ref, k_ref, vt_ref, o_ref, *scratch):
    i = pl.program_id(2)
    tq = qt_ref.shape[3]
    tk = vt_ref.shape[4]
    nh = qt_ref.shape[1]
    assert tq == 2 * tk
    s_refs = [scratch[2 * hh:2 * hh + 2] for hh in range(nh)]
    p_refs = [scratch[2 * nh + 2 * hh:2 * nh + 2 * hh + 2] for hh in range(nh)]
    acc_refs = scratch[4 * nh:5 * nh]
    ones = jnp.ones((ONES_ROWS, tk), BF16)
    heads = range(nh)

    def qk(hh, blk, par):
        kblk = k_ref[0, hh, pl.ds(pl.multiple_of(blk * tk, tk), tk), :]
        s = _dot(kblk, qt_ref[0, hh])
        s_refs[hh][par][...] = s
        return jnp.max(s, axis=0, keepdims=True)

    def softmax(hh, par, m, cmax):
        m_new = jnp.maximum(m, cmax)
        p_refs[hh][par][...] = jnp.exp2(s_refs[hh][par][...] - m_new).astype(BF16)
        return m_new, jnp.exp2(m - m_new)

    def pv(hh, blk, par, alpha):
        vaug = jnp.concatenate([vt_ref[0, hh, blk], ones], axis=0)
        acc_refs[hh][...] = alpha * acc_refs[hh][...] + _dot(vaug, p_refs[hh][par][...])

    def pair(jj, carry):
        ms, alphas, cm0, cm1 = carry
        a = 2 * jj
        ms, alphas_a = zip(*[softmax(hh, 0, ms[hh], cm0[hh]) for hh in heads])
        for hh in heads:
            pv(hh, jnp.maximum(a - 1, 0), 1, alphas[hh])
        cm0 = [qk(hh, a + 2, 0) for hh in heads]
        ms, alphas_b = zip(*[softmax(hh, 1, ms[hh], cm1[hh]) for hh in heads])
        for hh in heads:
            pv(hh, a, 0, alphas_a[hh])
        cm1 = [qk(hh, a + 3, 1) for hh in heads]
        return list(ms), list(alphas_b), cm0, cm1

    for hh in heads:
        p_refs[hh][1][...] = jnp.zeros((tk, tq), BF16)
        acc_refs[hh][...] = jnp.zeros((V_DIM + ONES_ROWS, tq), F32)
    cm0 = [qk(hh, 0, 0) for hh in heads]
    cm1 = [qk(hh, 1, 1) for hh in heads]
    ms = [jnp.full((1, tq), -jnp.inf, F32) for _ in heads]
    alphas = [jnp.ones((1, tq), F32) for _ in heads]
    ms, alphas, cm0, _ = jax.lax.fori_loop(0, i, pair, (ms, alphas, cm0, cm1))

    d0 = 2 * i
    delta = (jax.lax.broadcasted_iota(jnp.int32, (tk, tk), 0)
             - jax.lax.broadcasted_iota(jnp.int32, (tk, tk), 1))
    thr = i * tq - d0 * tk
    lo, hi = slice(0, tk), slice(tk, tq)

    def masked_softmax(hh, par, cols, m):
        s = jnp.where(delta <= thr, s_refs[hh][par][:, cols], -jnp.inf)
        m_new = jnp.maximum(m, jnp.max(s, axis=0, keepdims=True))
        p_refs[hh][par][:, cols] = jnp.exp2(s - m_new).astype(BF16)
        return m_new, jnp.exp2(m - m_new)

    alphas_a = []
    for hh in heads:
        m_lo, a_lo = masked_softmax(hh, 0, lo, ms[hh][:, lo])
        m_hi = jnp.maximum(ms[hh][:, hi], cm0[hh][:, hi])
        p_refs[hh][0][:, hi] = jnp.exp2(s_refs[hh][0][:, hi] - m_hi).astype(BF16)
        a_hi = jnp.exp2(ms[hh][:, hi] - m_hi)
        ms[hh] = jnp.concatenate([m_lo, m_hi], axis=1)
        alphas_a.append(jnp.concatenate([a_lo, a_hi], axis=1))
    for hh in heads:
        pv(hh, jnp.maximum(d0 - 1, 0), 1, alphas[hh])
    alphas_b = [masked_softmax(hh, 1, hi, ms[hh][:, hi])[1] for hh in heads]
    for hh in heads:
        pv(hh, d0, 0, alphas_a[hh])
    for hh in heads:
        vaug = jnp.concatenate([vt_ref[0, hh, d0 + 1], ones], axis=0)
        acc_refs[hh][:, hi] = (alphas_b[hh] * acc_refs[hh][:, hi]
                               + _dot(vaug, p_refs[hh][1][:, hi]))
    outs = [acc_refs[hh][:V_DIM, :] / acc_refs[hh][V_DIM:V_DIM + 1, :] for hh in heads]
    o_ref[0] = jnp.concatenate(outs, axis=0).T.astype(BF16)


def _mla(qbt, kb, vbt, *, tq, nh=2):
    B, H, _, S = qbt.shape
    tk = vbt.shape[4]
    return pl.pallas_call(
        _mla_kernel,
        out_shape=jax.ShapeDtypeStruct((B, S, H * V_DIM), BF16),
        grid=(B, H // nh, S // tq),
        in_specs=[
            pl.BlockSpec((1, nh, HEAD_PAD, tq), lambda b, hp, i: (b, hp, 0, i)),
            pl.BlockSpec((1, nh, S, HEAD_PAD), lambda b, hp, i: (b, hp, 0, 0)),
            pl.BlockSpec((1, nh, S // tk, V_DIM, tk), lambda b, hp, i: (b, hp, 0, 0, 0)),
        ],
        out_specs=pl.BlockSpec((1, tq, nh * V_DIM), lambda b, hp, i: (b, i, hp)),
        scratch_shapes=([pltpu.VMEM((tk, tq), F32)] * (2 * nh)
                        + [pltpu.VMEM((tk, tq), BF16)] * (2 * nh)
                        + [pltpu.VMEM((V_DIM + ONES_ROWS, tq), F32)] * nh),
        compiler_params=pltpu.CompilerParams(
            dimension_semantics=("parallel", "parallel", "arbitrary"),
            vmem_limit_bytes=V7X_VMEM_LIMIT),
        name="mla",
    )(qbt, kb, vbt)


def _merge_kernel(x_ref, oa_ref, ob_ref, g_ref, wg_ref, wa_ref, wb_ref, wo_ref, o_ref):
    x = x_ref[0]
    h = _rms(x, g_ref[...]).astype(BF16)
    gates = jax.nn.sigmoid(_dot(h, wg_ref[...]))
    ya = _dot(oa_ref[0], wa_ref[...])
    yb = _dot(ob_ref[0], wb_ref[...])
    merged = gates[:, :D_MODEL] * ya + gates[:, D_MODEL:] * yb
    o_ref[0] = x + _dot(merged.astype(BF16), wo_ref[...])


def _merge(x, oa, ob, g, wg, wa, wb, wo, *, tm):
    B, S, _ = x.shape
    full = lambda shape: pl.BlockSpec(shape, lambda b, i: (0,) * len(shape))
    tok = lambda w: pl.BlockSpec((1, tm, w), lambda b, i: (b, i, 0))
    return pl.pallas_call(
        _merge_kernel,
        out_shape=jax.ShapeDtypeStruct(x.shape, F32),
        grid=(B, S // tm),
        in_specs=[tok(D_MODEL), tok(A_Q_W), tok(B_HEADS * V_DIM), full(g.shape),
                  full(wg.shape), full(wa.shape), full(wb.shape), full(wo.shape)],
        out_specs=tok(D_MODEL),
        compiler_params=pltpu.CompilerParams(
            dimension_semantics=("parallel", "parallel"),
            vmem_limit_bytes=V7X_VMEM_LIMIT),
        name="merge",
    )(x, oa, ob, g, wg, wa, wb, wo)


def _mlp_kernel(x_ref, g_ref, w1_ref, w2_ref, gf_ref, o_ref, *, final):
    x = x_ref[0]
    h = _rms(x, g_ref[...]).astype(BF16)
    u = jnp.square(jnp.maximum(_dot(h, w1_ref[...]), 0.0)).astype(BF16)
    y = x + _dot(u, w2_ref[...])
    if final:
        y = _rms(y, gf_ref[...])
    o_ref[0] = y


def _mlp(x, g, w1, w2, gf, *, tm, final):
    B, S, _ = x.shape
    full = lambda shape: pl.BlockSpec(shape, lambda b, i: (0,) * len(shape))
    tok = pl.BlockSpec((1, tm, D_MODEL), lambda b, i: (b, i, 0))
    return pl.pallas_call(
        functools.partial(_mlp_kernel, final=final),
        out_shape=jax.ShapeDtypeStruct(x.shape, F32),
        grid=(B, S // tm),
        in_specs=[tok, full(g.shape), full(w1.shape), full(w2.shape), full(gf.shape)],
        out_specs=tok,
        compiler_params=pltpu.CompilerParams(
            dimension_semantics=("parallel", "parallel"),
            vmem_limit_bytes=V7X_VMEM_LIMIT),
        name="mlp_final" if final else "mlp",
    )(x, g, w1, w2, gf)


def _prep_layer(w_in, w_uq, w_ukv):
    o = 0
    wqa = w_in[:, o:o + A_Q_W]; o += A_Q_W
    wka = w_in[:, o:o + A_KV_W]; o += A_KV_W
    wva = w_in[:, o:o + A_KV_W]; o += A_KV_W
    wcq = w_in[:, o:o + Q_LORA]; o += Q_LORA
    wckv = w_in[:, o:o + KV_LORA]; o += KV_LORA
    wkr = w_in[:, o:o + ROPE_DIM]; o += ROPE_DIM
    wg = w_in[:, o:]
    wkr_pad = jnp.pad(wkr, ((0, 0), (NOPE_DIM, HEAD_PAD - NOPE_DIM - ROPE_DIM)))
    w1 = jnp.concatenate([wqa, wka, wva, wcq, wckv, wkr_pad], axis=1).astype(BF16)
    wuq = w_uq.reshape(Q_LORA, B_HEADS, B_QK_DIM)
    wuq = jnp.pad(wuq, ((0, 0), (0, 0), (0, HEAD_PAD - B_QK_DIM)))
    wuq = wuq.reshape(Q_LORA, B_HEADS * HEAD_PAD).astype(BF16)
    wukv = w_ukv.reshape(KV_LORA, B_HEADS, NOPE_DIM + V_DIM)
    wk = jnp.pad(wukv[:, :, :NOPE_DIM], ((0, 0), (0, 0), (0, HEAD_PAD - NOPE_DIM)))
    wk = wk.reshape(KV_LORA, B_HEADS * HEAD_PAD).astype(BF16)
    wv = wukv[:, :, NOPE_DIM:].reshape(KV_LORA, B_HEADS * V_DIM).astype(BF16)
    return w1, wg.astype(BF16), wuq, wk, wv


def kernel(x, positions, rel_bias_table, norm_mix, w_in, attn_sinks, q_norm, kv_norm,
           w_uq, w_ukv, w_branch_a, w_branch_b, w_out, norm_mlp, w_ff1, w_ff2, norm_final):
    B, S, _ = x.shape
    depth = w_in.shape[0]
    tm = min(TOK_TILE, S)
    t = min(MLA_TILE, S)
    tk = MLA_KEYS
    ta = min(SWA_TILE, S)
    assert S % tm == 0 and S % t == 0 and S % ta == 0 and tm % tk == 0 and t == 2 * tk and ta % BLOCK == 0

    half = ROPE_DIM // 2
    inv_freq = ROPE_THETA ** (-jnp.arange(half, dtype=F32) / half)
    invf = inv_freq.reshape(half, 1)
    pos3 = positions.reshape(B, 1, S)
    bias = _rel_bias(rel_bias_table.astype(F32))
    row = lambda v: v.reshape(1, -1).astype(F32)

    for l in range(depth):
        w1, wg, wuq, wk, wv = _prep_layer(w_in[l], w_uq[l], w_ukv[l])
        qat, ka, vat, qbt, kb, vbt = _in_proj(
            x, pos3, invf, row(norm_mix[l]), w1, row(q_norm[l]), row(kv_norm[l]),
            wuq, wk, wv, tm=tm, tk=tk)
        sink_rows = jnp.repeat(attn_sinks[l].astype(F32).reshape(A_KV_HEADS, 1, -1) * LOG2E,
                               BLOCK, axis=2)
        oa = _swa(sink_rows, qat, ka, vat, bias, ta=ta)
        ob = _mla(qbt, kb, vbt, tq=t)
        x = _merge(x, oa, ob, row(norm_mix[l]), wg, w_branch_a[l].astype(BF16),
                   w_branch_b[l].astype(BF16), w_out[l].astype(BF16), tm=tm)
        x = _mlp(x, row(norm_mlp[l]), w_ff1[l].astype(BF16), w_ff2[l].astype(BF16),
                 row(norm_final), tm=tm, final=(l == depth - 1))
    return x
```

```python
import functools
import math

import jax
import jax.numpy as jnp
import numpy as np
from jax.experimental import pallas as pl
from jax.experimental.pallas import tpu as pltpu

D_MODEL = 1024
A_HEADS = 8
A_KV_HEADS = 2
A_HEAD_DIM = 64
WINDOW = 128
BLOCK = 128
B_HEADS = 8
Q_LORA = 384
KV_LORA = 256
NOPE_DIM = 64
ROPE_DIM = 32
V_DIM = 64
ROPE_THETA = 10000.0
N_BUCKETS = 32
MAX_DISTANCE = 128
D_FF = 4 * D_MODEL
EPS = 1e-5
A_Q_W = A_HEADS * A_HEAD_DIM
A_KV_W = A_KV_HEADS * A_HEAD_DIM
B_QK_DIM = NOPE_DIM + ROPE_DIM

LANES = 128
V7X_VMEM_LIMIT = 56 * 1024 * 1024

HEAD_PAD = LANES
ONES_ROWS = 16
TOK_TILE = 512
MLA_TILE = 512
MLA_KEYS = 256
SWA_TILE = 512

BF16 = jnp.bfloat16
F32 = jnp.float32
LOG2E = math.log2(math.e)


def _rms(x, g):
    return x * jax.lax.rsqrt(jnp.mean(x * x, axis=-1, keepdims=True) + EPS) * g


def _dot(a, b):
    return jnp.dot(a, b, preferred_element_type=F32)


def _in_proj_kernel(x_ref, pos_ref, invf_ref, g_ref, w1_ref, qn_ref, kvn_ref,
                    wuq_ref, wk_ref, wv_ref,
                    qat_ref, ka_ref, vat_ref, qbt_ref, kb_ref, vbt_ref, *, b_scale):
    tm = x_ref.shape[1]
    h = _rms(x_ref[0], g_ref[...]).astype(BF16)
    z = _dot(h, w1_ref[...])
    o = 0
    qa = z[:, o:o + A_Q_W]; o += A_Q_W
    ka = z[:, o:o + A_KV_W]; o += A_KV_W
    va = z[:, o:o + A_KV_W]; o += A_KV_W
    cq = z[:, o:o + Q_LORA]; o += Q_LORA
    ckv = z[:, o:o + KV_LORA]; o += KV_LORA
    kr = z[:, o:o + HEAD_PAD]

    qat_ref[0] = (qa * (LOG2E / math.sqrt(A_HEAD_DIM))).T.astype(BF16)
    ka_ref[0] = ka.astype(BF16)
    vat_ref[0] = va.T.astype(BF16)

    cqn = _rms(cq, qn_ref[...]).astype(BF16)
    qb = _dot(cqn, wuq_ref[...]) * b_scale
    ckvn = _rms(ckv, kvn_ref[...]).astype(BF16)
    kb = _dot(ckvn, wk_ref[...])
    vb = _dot(ckvn, wv_ref[...])

    ang = pos_ref[0].astype(F32) * invf_ref[...]
    cos, sin = jnp.cos(ang), jnp.sin(ang)
    half = ROPE_DIM // 2
    r0, r1, r2 = NOPE_DIM, NOPE_DIM + half, NOPE_DIM + ROPE_DIM

    def rope_t(t):
        t1, t2 = t[r0:r1], t[r1:r2]
        return jnp.concatenate(
            [t[:r0], t1 * cos - t2 * sin, t2 * cos + t1 * sin, t[r2:]], axis=0)

    tq = qbt_ref.shape[4]
    for hd in range(B_HEADS):
        qt = rope_t(qb[:, hd * HEAD_PAD:(hd + 1) * HEAD_PAD].T).astype(BF16)
        for s in range(tm // tq):
            qbt_ref[0, hd, s] = qt[:, s * tq:(s + 1) * tq]

    kr_rot = rope_t(kr.T).T
    for hd in range(B_HEADS):
        kb_ref[0, hd] = (kb[:, hd * HEAD_PAD:(hd + 1) * HEAD_PAD] + kr_rot).astype(BF16)

    vbt = vb.T.astype(BF16)
    nsub = vbt_ref.shape[2]
    tk = vbt_ref.shape[4]
    for hd in range(B_HEADS):
        for s in range(nsub):
            vbt_ref[0, hd, s] = vbt[hd * V_DIM:(hd + 1) * V_DIM, s * tk:(s + 1) * tk]


def _in_proj(x, pos3, invf, g, w1, qn, kvn, wuq, wk, wv, *, tm, tq, tk):
    B, S, _ = x.shape
    nsub = tm // tk
    grid = (B, S // tm)
    full = lambda shape: pl.BlockSpec(shape, lambda b, i: (0,) * len(shape))
    out_shape = (
        jax.ShapeDtypeStruct((B, A_Q_W, S), BF16),
        jax.ShapeDtypeStruct((B, S, A_KV_W), BF16),
        jax.ShapeDtypeStruct((B, A_KV_W, S), BF16),
        jax.ShapeDtypeStruct((B, B_HEADS, S // tq, HEAD_PAD, tq), BF16),
        jax.ShapeDtypeStruct((B, B_HEADS, S, HEAD_PAD), BF16),
        jax.ShapeDtypeStruct((B, B_HEADS, S // tk, V_DIM, tk), BF16),
    )
    out_specs = (
        pl.BlockSpec((1, A_Q_W, tm), lambda b, i: (b, 0, i)),
        pl.BlockSpec((1, tm, A_KV_W), lambda b, i: (b, i, 0)),
        pl.BlockSpec((1, A_KV_W, tm), lambda b, i: (b, 0, i)),
        pl.BlockSpec((1, B_HEADS, tm // tq, HEAD_PAD, tq), lambda b, i: (b, 0, i, 0, 0)),
        pl.BlockSpec((1, B_HEADS, tm, HEAD_PAD), lambda b, i: (b, 0, i, 0)),
        pl.BlockSpec((1, B_HEADS, nsub, V_DIM, tk), lambda b, i: (b, 0, i, 0, 0)),
    )
    in_specs = [
        pl.BlockSpec((1, tm, D_MODEL), lambda b, i: (b, i, 0)),
        pl.BlockSpec((1, 1, tm), lambda b, i: (b, 0, i)),
        full(invf.shape), full(g.shape), full(w1.shape), full(qn.shape), full(kvn.shape),
        full(wuq.shape), full(wk.shape), full(wv.shape),
    ]
    return pl.pallas_call(
        functools.partial(_in_proj_kernel, b_scale=LOG2E / math.sqrt(B_QK_DIM)),
        out_shape=out_shape, grid=grid, in_specs=in_specs, out_specs=out_specs,
        compiler_params=pltpu.CompilerParams(
            dimension_semantics=("parallel", "parallel"),
            vmem_limit_bytes=V7X_VMEM_LIMIT),
        name="in_proj",
    )(x, pos3, invf, g, w1, qn, kvn, wuq, wk, wv)


def _bias_kernel(table_ref, bkt_ref, out_ref):
    bkt = bkt_ref[...]
    row = jax.lax.broadcasted_iota(jnp.int32, bkt.shape, 0)
    group = A_HEADS // A_KV_HEADS
    for hd in range(A_HEADS):
        acc = jnp.full(bkt.shape, -jnp.inf, F32)
        for b in range(N_BUCKETS):
            acc = jnp.where(bkt == b, table_ref[b, hd] * LOG2E, acc)
        g, lane = hd // group, (hd % group) * BLOCK
        out_ref[0, g, :, lane:lane + BLOCK] = acc
        out_ref[1, g, :, lane:lane + BLOCK] = jnp.where(row < BLOCK, -jnp.inf, acc)


def _bucket_table():
    kj = np.arange(2 * BLOCK)[:, None]
    qi = np.arange(BLOCK)[None, :]
    dist = BLOCK + qi - kj
    max_exact = N_BUCKETS // 2
    n = np.maximum(dist, 0)
    nf = np.maximum(n, 1).astype(np.float32)
    large = max_exact + (np.log(nf / max_exact) / math.log(MAX_DISTANCE / max_exact)
                         * (N_BUCKETS - max_exact)).astype(np.int32)
    large = np.minimum(large, N_BUCKETS - 1)
    bkt = np.where(n < max_exact, n, large)
    in_win = (dist >= 0) & (dist < WINDOW)
    return np.where(in_win, bkt, -1).astype(np.int32)


def _rel_bias(table):
    bkt = jnp.asarray(_bucket_table())
    return pl.pallas_call(
        _bias_kernel,
        out_shape=jax.ShapeDtypeStruct(
            (2, A_KV_HEADS, 2 * BLOCK, (A_HEADS // A_KV_HEADS) * BLOCK), F32),
        in_specs=[pl.BlockSpec(memory_space=pltpu.SMEM),
                  pl.BlockSpec(memory_space=pltpu.VMEM)],
        out_specs=pl.BlockSpec(memory_space=pltpu.VMEM),
        name="rel_bias",
    )(table, bkt)


def _swa_kernel(sink_ref, qt_ref, kc_ref, kp_ref, vc_ref, vp_ref, bias_ref, o_ref):
    i = pl.program_id(1)
    ta = qt_ref.shape[2]
    group = A_HEADS // A_KV_HEADS
    width = group * BLOCK
    kcat = jnp.concatenate([kp_ref[0], kc_ref[0]], axis=0)
    vcat = jnp.concatenate([vp_ref[0], vc_ref[0]], axis=1)
    ones = jnp.ones((ONES_ROWS, 2 * BLOCK), BF16)
    zpad = jnp.zeros((A_HEAD_DIM, width), BF16)
    units = [(r, g) for r in range(ta // BLOCK) for g in range(A_KV_HEADS)]

    def scores(r, g):
        kk = kcat[r * BLOCK:(r + 2) * BLOCK]
        qrow = jnp.concatenate(
            [qt_ref[0, hd * A_HEAD_DIM:(hd + 1) * A_HEAD_DIM, r * BLOCK:(r + 1) * BLOCK]
             for hd in range(g * group, (g + 1) * group)], axis=1)
        qpad = jnp.concatenate([qrow, zpad] if g == 0 else [zpad, qrow], axis=0)
        variant = jnp.where(i == 0, 1, 0) if r == 0 else 0
        return _dot(kk, qpad) + bias_ref[variant, g]

    def attend(r, g, s):
        sink = sink_ref[g]
        m = jnp.maximum(jnp.max(s, axis=0, keepdims=True), sink)
        p = jnp.exp2(s - m).astype(BF16)
        vv = vcat[g * A_HEAD_DIM:(g + 1) * A_HEAD_DIM, r * BLOCK:(r + 2) * BLOCK]
        acc = _dot(jnp.concatenate([vv, ones], axis=0), p)
        denom = acc[A_HEAD_DIM:A_HEAD_DIM + 1] + jnp.exp2(sink - m)
        return acc[:A_HEAD_DIM] / denom

    outs = {}
    s_next = scores(*units[0])
    for n, (r, g) in enumerate(units):
        s = s_next
        if n + 1 < len(units):
            s_next = scores(*units[n + 1])
        outs[(r, g)] = attend(r, g, s)
    for r in range(ta // BLOCK):
        heads = [outs[(r, g)][:, hh * BLOCK:(hh + 1) * BLOCK]
                 for g in range(A_KV_HEADS) for hh in range(group)]
        o_ref[0, r * BLOCK:(r + 1) * BLOCK, :] = jnp.concatenate(heads, axis=0).T.astype(BF16)


def _swa(sink_rows, qat, ka, vat, bias, *, ta):
    B, _, S = qat.shape
    sub = ta // BLOCK
    prev = lambda i: jnp.maximum(i * sub - 1, 0)
    return pl.pallas_call(
        _swa_kernel,
        out_shape=jax.ShapeDtypeStruct((B, S, A_Q_W), BF16),
        grid=(B, S // ta),
        in_specs=[
            pl.BlockSpec(sink_rows.shape, lambda b, i: (0, 0, 0)),
            pl.BlockSpec((1, A_Q_W, ta), lambda b, i: (b, 0, i)),
            pl.BlockSpec((1, ta, A_KV_W), lambda b, i: (b, i, 0)),
            pl.BlockSpec((1, BLOCK, A_KV_W), lambda b, i: (b, prev(i), 0)),
            pl.BlockSpec((1, A_KV_W, ta), lambda b, i: (b, 0, i)),
            pl.BlockSpec((1, A_KV_W, BLOCK), lambda b, i: (b, 0, prev(i))),
            pl.BlockSpec(bias.shape, lambda b, i: (0, 0, 0, 0)),
        ],
        out_specs=pl.BlockSpec((1, ta, A_Q_W), lambda b, i: (b, i, 0)),
        compiler_params=pltpu.CompilerParams(
            dimension_semantics=("parallel", "parallel"),
            vmem_limit_bytes=V7X_VMEM_LIMIT),
        name="swa",
    )(sink_rows, qat, ka, ka, vat, vat, bias)


def _mla_kernel(qt_ref, k_ref, vt_ref, o_ref, *scratch):
    i = pl.program_id(2)
    n_tiles = qt_ref.shape[2]
    tq = qt_ref.shape[4]
    tk = vt_ref.shape[4]
    nh = qt_ref.shape[1]
    assert tq == 2 * tk
    s_refs = [scratch[2 * hh:2 * hh + 2] for hh in range(nh)]
    p_refs = [scratch[2 * nh + 2 * hh:2 * nh + 2 * hh + 2] for hh in range(nh)]
    acc_refs = scratch[4 * nh:5 * nh]
    cm_ref = scratch[5 * nh]
    ones = jnp.ones((ONES_ROWS, tk), BF16)
    heads = range(nh)

    def qk(hh, blk, par, tile=i):
        kblk = k_ref[0, hh, pl.ds(pl.multiple_of(blk * tk, tk), tk), :]
        s = _dot(kblk, qt_ref[0, hh, tile])
        s_refs[hh][par][...] = s
        return jnp.max(s, axis=0, keepdims=True)

    def first_scores(tile):
        for hh in heads:
            for par in range(2):
                cm_ref[2 * hh + par, 0:1, :] = qk(hh, par, par, tile)

    def softmax(hh, par, m, cmax):
        m_new = jnp.maximum(m, cmax)
        p_refs[hh][par][...] = jnp.exp2(s_refs[hh][par][...] - m_new).astype(BF16)
        return m_new, jnp.exp2(m - m_new)

    def pv(hh, blk, par, alpha):
        vaug = jnp.concatenate([vt_ref[0, hh, blk], ones], axis=0)
        acc_refs[hh][...] = alpha * acc_refs[hh][...] + _dot(vaug, p_refs[hh][par][...])

    def pair(jj, carry):
        ms, alphas, cm0, cm1 = carry
        a = 2 * jj
        ms, alphas_a = zip(*[softmax(hh, 0, ms[hh], cm0[hh]) for hh in heads])
        for hh in heads:
            pv(hh, jnp.maximum(a - 1, 0), 1, alphas[hh])
        cm0 = [qk(hh, a + 2, 0) for hh in heads]
        ms, alphas_b = zip(*[softmax(hh, 1, ms[hh], cm1[hh]) for hh in heads])
        for hh in heads:
            pv(hh, a, 0, alphas_a[hh])
        cm1 = [qk(hh, a + 3, 1) for hh in heads]
        return list(ms), list(alphas_b), cm0, cm1

    for hh in heads:
        p_refs[hh][1][...] = jnp.zeros((tk, tq), BF16)
        acc_refs[hh][...] = jnp.zeros((V_DIM + ONES_ROWS, tq), F32)

    @pl.when(i == 0)
    def _():
        first_scores(0)

    cm0 = [cm_ref[2 * hh, 0:1, :] for hh in heads]
    cm1 = [cm_ref[2 * hh + 1, 0:1, :] for hh in heads]
    ms = [jnp.full((1, tq), -jnp.inf, F32) for _ in heads]
    alphas = [jnp.ones((1, tq), F32) for _ in heads]

    def quad(qq, carry):
        return pair(2 * qq + 1, pair(2 * qq, carry))

    carry = jax.lax.fori_loop(0, i // 2, quad, (ms, alphas, cm0, cm1))
    carry = jax.lax.cond(i % 2 == 1, lambda c: pair(i - 1, c), lambda c: c, carry)
    ms, alphas, cm0, _ = carry

    d0 = 2 * i
    delta = (jax.lax.broadcasted_iota(jnp.int32, (tk, tk), 0)
             - jax.lax.broadcasted_iota(jnp.int32, (tk, tk), 1))
    thr = i * tq - d0 * tk
    lo, hi = slice(0, tk), slice(tk, tq)

    def masked_softmax(hh, par, cols, m):
        s = jnp.where(delta <= thr, s_refs[hh][par][:, cols], -jnp.inf)
        m_new = jnp.maximum(m, jnp.max(s, axis=0, keepdims=True))
        p_refs[hh][par][:, cols] = jnp.exp2(s - m_new).astype(BF16)
        return m_new, jnp.exp2(m - m_new)

    alphas_a = []
    for hh in heads:
        m_lo, a_lo = masked_softmax(hh, 0, lo, ms[hh][:, lo])
        m_hi = jnp.maximum(ms[hh][:, hi], cm0[hh][:, hi])
        p_refs[hh][0][:, hi] = jnp.exp2(s_refs[hh][0][:, hi] - m_hi).astype(BF16)
        a_hi = jnp.exp2(ms[hh][:, hi] - m_hi)
        ms[hh] = jnp.concatenate([m_lo, m_hi], axis=1)
        alphas_a.append(jnp.concatenate([a_lo, a_hi], axis=1))
    for hh in heads:
        pv(hh, jnp.maximum(d0 - 1, 0), 1, alphas[hh])
    alphas_b = [masked_softmax(hh, 1, hi, ms[hh][:, hi])[1] for hh in heads]
    first_scores(jnp.minimum(i + 1, n_tiles - 1))
    for hh in heads:
        pv(hh, d0, 0, alphas_a[hh])
    for hh in heads:
        vaug = jnp.concatenate([vt_ref[0, hh, d0 + 1], ones], axis=0)
        acc_refs[hh][:, hi] = (alphas_b[hh] * acc_refs[hh][:, hi]
                               + _dot(vaug, p_refs[hh][1][:, hi]))
    outs = [acc_refs[hh][:V_DIM, :] / acc_refs[hh][V_DIM:V_DIM + 1, :] for hh in heads]
    o_ref[0] = jnp.concatenate(outs, axis=0).T.astype(BF16)


def _mla(qbt, kb, vbt, *, nh=2):
    B, H, n_tiles, _, tq = qbt.shape
    S = n_tiles * tq
    tk = vbt.shape[4]
    return pl.pallas_call(
        _mla_kernel,
        out_shape=jax.ShapeDtypeStruct((B, S, H * V_DIM), BF16),
        grid=(B, H // nh, n_tiles),
        in_specs=[
            pl.BlockSpec((1, nh, n_tiles, HEAD_PAD, tq), lambda b, hp, i: (b, hp, 0, 0, 0)),
            pl.BlockSpec((1, nh, S, HEAD_PAD), lambda b, hp, i: (b, hp, 0, 0)),
            pl.BlockSpec((1, nh, S // tk, V_DIM, tk), lambda b, hp, i: (b, hp, 0, 0, 0)),
        ],
        out_specs=pl.BlockSpec((1, tq, nh * V_DIM), lambda b, hp, i: (b, i, hp)),
        scratch_shapes=([pltpu.VMEM((tk, tq), F32)] * (2 * nh)
                        + [pltpu.VMEM((tk, tq), BF16)] * (2 * nh)
                        + [pltpu.VMEM((V_DIM + ONES_ROWS, tq), F32)] * nh
                        + [pltpu.VMEM((2 * nh, 8, tq), F32)]),
        compiler_params=pltpu.CompilerParams(
            dimension_semantics=("parallel", "parallel", "arbitrary"),
            vmem_limit_bytes=V7X_VMEM_LIMIT),
        name="mla",
    )(qbt, kb, vbt)


def _merge_kernel(x_ref, oa_ref, ob_ref, g_ref, wg_ref, wa_ref, wb_ref, wo_ref, o_ref):
    x = x_ref[0]
    h = _rms(x, g_ref[...]).astype(BF16)
    gates = jax.nn.sigmoid(_dot(h, wg_ref[...]))
    ya = _dot(oa_ref[0], wa_ref[...])
    yb = _dot(ob_ref[0], wb_ref[...])
    merged = gates[:, :D_MODEL] * ya + gates[:, D_MODEL:] * yb
    o_ref[0] = x + _dot(merged.astype(BF16), wo_ref[...])


def _merge(x, oa, ob, g, wg, wa, wb, wo, *, tm):
    B, S, _ = x.shape
    full = lambda shape: pl.BlockSpec(shape, lambda b, i: (0,) * len(shape))
    tok = lambda w: pl.BlockSpec((1, tm, w), lambda b, i: (b, i, 0))
    return pl.pallas_call(
        _merge_kernel,
        out_shape=jax.ShapeDtypeStruct(x.shape, F32),
        grid=(B, S // tm),
        in_specs=[tok(D_MODEL), tok(A_Q_W), tok(B_HEADS * V_DIM), full(g.shape),
                  full(wg.shape), full(wa.shape), full(wb.shape), full(wo.shape)],
        out_specs=tok(D_MODEL),
        compiler_params=pltpu.CompilerParams(
            dimension_semantics=("parallel", "parallel"),
            vmem_limit_bytes=V7X_VMEM_LIMIT),
        name="merge",
    )(x, oa, ob, g, wg, wa, wb, wo)


def _mlp_kernel(x_ref, g_ref, w1_ref, w2_ref, gf_ref, o_ref, *, final):
    x = x_ref[0]
    h = _rms(x, g_ref[...]).astype(BF16)
    u = jnp.square(jnp.maximum(_dot(h, w1_ref[...]), 0.0)).astype(BF16)
    y = x + _dot(u, w2_ref[...])
    if final:
        y = _rms(y, gf_ref[...])
    o_ref[0] = y


def _mlp(x, g, w1, w2, gf, *, tm, final):
    B, S, _ = x.shape
    full = lambda shape: pl.BlockSpec(shape, lambda b, i: (0,) * len(shape))
    tok = pl.BlockSpec((1, tm, D_MODEL), lambda b, i: (b, i, 0))
    return pl.pallas_call(
        functools.partial(_mlp_kernel, final=final),
        out_shape=jax.ShapeDtypeStruct(x.shape, F32),
        grid=(B, S // tm),
        in_specs=[tok, full(g.shape), full(w1.shape), full(w2.shape), full(gf.shape)],
        out_specs=tok,
        compiler_params=pltpu.CompilerParams(
            dimension_semantics=("parallel", "parallel"),
            vmem_limit_bytes=V7X_VMEM_LIMIT),
        name="mlp_final" if final else "mlp",
    )(x, g, w1, w2, gf)


def _prep_layer(w_in, w_uq, w_ukv):
    o = 0
    wqa = w_in[:, o:o + A_Q_W]; o += A_Q_W
    wka = w_in[:, o:o + A_KV_W]; o += A_KV_W
    wva = w_in[:, o:o + A_KV_W]; o += A_KV_W
    wcq = w_in[:, o:o + Q_LORA]; o += Q_LORA
    wckv = w_in[:, o:o + KV_LORA]; o += KV_LORA
    wkr = w_in[:, o:o + ROPE_DIM]; o += ROPE_DIM
    wg = w_in[:, o:]
    wkr_pad = jnp.pad(wkr, ((0, 0), (NOPE_DIM, HEAD_PAD - NOPE_DIM - ROPE_DIM)))
    w1 = jnp.concatenate([wqa, wka, wva, wcq, wckv, wkr_pad], axis=1).astype(BF16)
    wuq = w_uq.reshape(Q_LORA, B_HEADS, B_QK_DIM)
    wuq = jnp.pad(wuq, ((0, 0), (0, 0), (0, HEAD_PAD - B_QK_DIM)))
    wuq = wuq.reshape(Q_LORA, B_HEADS * HEAD_PAD).astype(BF16)
    wukv = w_ukv.reshape(KV_LORA, B_HEADS, NOPE_DIM + V_DIM)
    wk = jnp.pad(wukv[:, :, :NOPE_DIM], ((0, 0), (0, 0), (0, HEAD_PAD - NOPE_DIM)))
    wk = wk.reshape(KV_LORA, B_HEADS * HEAD_PAD).astype(BF16)
    wv = wukv[:, :, NOPE_DIM:].reshape(KV_LORA, B_HEADS * V_DIM).astype(BF16)
    return w1, wg.astype(BF16), wuq, wk, wv


def kernel(x, positions, rel_bias_table, norm_mix, w_in, attn_sinks, q_norm, kv_norm,
           w_uq, w_ukv, w_branch_a, w_branch_b, w_out, norm_mlp, w_ff1, w_ff2, norm_final):
    B, S, _ = x.shape
    depth = w_in.shape[0]
    tm = min(TOK_TILE, S)
    t = min(MLA_TILE, S)
    tk = MLA_KEYS
    ta = min(SWA_TILE, S)
    assert S % tm == 0 and S % ta == 0 and tm % t == 0 and t == 2 * tk and ta % BLOCK == 0

    half = ROPE_DIM // 2
    inv_freq = ROPE_THETA ** (-jnp.arange(half, dtype=F32) / half)
    invf = inv_freq.reshape(half, 1)
    pos3 = positions.reshape(B, 1, S)
    bias = _rel_bias(rel_bias_table.astype(F32))
    row = lambda v: v.reshape(1, -1).astype(F32)

    for l in range(depth):
        w1, wg, wuq, wk, wv = _prep_layer(w_in[l], w_uq[l], w_ukv[l])
        qat, ka, vat, qbt, kb, vbt = _in_proj(
            x, pos3, invf, row(norm_mix[l]), w1, row(q_norm[l]), row(kv_norm[l]),
            wuq, wk, wv, tm=tm, tq=t, tk=tk)
        sink_rows = jnp.repeat(attn_sinks[l].astype(F32).reshape(A_KV_HEADS, 1, -1) * LOG2E,
                               BLOCK, axis=2)
        oa = _swa(sink_rows, qat, ka, vat, bias, ta=ta)
        ob = _mla(qbt, kb, vbt)
        x = _merge(x, oa, ob, row(norm_mix[l]), wg, w_branch_a[l].astype(BF16),
                   w_branch_b[l].astype(BF16), w_out[l].astype(BF16), tm=tm)
        x = _mlp(x, row(norm_mlp[l]), w_ff1[l].astype(BF16), w_ff2[l].astype(BF16),
                 row(norm_final), tm=tm, final=(l == depth - 1))
    return x
```

```python
import functools
import math

import jax
import jax.numpy as jnp
import numpy as np
from jax.experimental import pallas as pl
from jax.experimental.pallas import tpu as pltpu

D_MODEL = 1024
A_HEADS = 8
A_KV_HEADS = 2
A_HEAD_DIM = 64
WINDOW = 128
BLOCK = 128
B_HEADS = 8
Q_LORA = 384
KV_LORA = 256
NOPE_DIM = 64
ROPE_DIM = 32
V_DIM = 64
ROPE_THETA = 10000.0
N_BUCKETS = 32
MAX_DISTANCE = 128
D_FF = 4 * D_MODEL
EPS = 1e-5
A_Q_W = A_HEADS * A_HEAD_DIM
A_KV_W = A_KV_HEADS * A_HEAD_DIM
B_QK_DIM = NOPE_DIM + ROPE_DIM

LANES = 128
V7X_VMEM_LIMIT = 56 * 1024 * 1024

HEAD_PAD = LANES
ONES_ROWS = 16
TOK_TILE = 512
MLA_TILE = 512
MLA_KEYS = 256
SWA_TILE = 512

BF16 = jnp.bfloat16
F32 = jnp.float32
LOG2E = math.log2(math.e)


def _rms(x, g):
    return x * jax.lax.rsqrt(jnp.mean(x * x, axis=-1, keepdims=True) + EPS) * g


def _dot(a, b):
    return jnp.dot(a, b, preferred_element_type=F32)


def _in_proj_kernel(x_ref, pos_ref, invf_ref, g_ref, w1_ref, qn_ref, kvn_ref,
                    wuq_ref, wk_ref, wv_ref,
                    qat_ref, ka_ref, vat_ref, qbt_ref, kb_ref, vbt_ref, *, b_scale):
    tm = x_ref.shape[1]
    h = _rms(x_ref[0], g_ref[...]).astype(BF16)
    swa_w = A_Q_W + 2 * A_KV_W
    zl = _dot(h, w1_ref[:, swa_w:])
    za = _dot(h, w1_ref[:, :swa_w])
    cq = zl[:, :Q_LORA]
    ckv = zl[:, Q_LORA:Q_LORA + KV_LORA]
    kr = zl[:, Q_LORA + KV_LORA:]
    qa = za[:, :A_Q_W]
    ka = za[:, A_Q_W:A_Q_W + A_KV_W]
    va = za[:, A_Q_W + A_KV_W:]

    qat_ref[0] = (qa * (LOG2E / math.sqrt(A_HEAD_DIM))).T.astype(BF16)
    ka_ref[0] = ka.astype(BF16)
    vat_ref[0] = va.T.astype(BF16)

    cqn = _rms(cq, qn_ref[...]).astype(BF16)
    qb = _dot(cqn, wuq_ref[...]) * b_scale
    ckvn = _rms(ckv, kvn_ref[...]).astype(BF16)
    kb = _dot(ckvn, wk_ref[...])
    vb = _dot(ckvn, wv_ref[...])

    ang = pos_ref[0].astype(F32) * invf_ref[...]
    cos, sin = jnp.cos(ang), jnp.sin(ang)
    half = ROPE_DIM // 2
    r0, r1, r2 = NOPE_DIM, NOPE_DIM + half, NOPE_DIM + ROPE_DIM

    def rope_t(t):
        t1, t2 = t[r0:r1], t[r1:r2]
        return jnp.concatenate(
            [t[:r0], t1 * cos - t2 * sin, t2 * cos + t1 * sin, t[r2:]], axis=0)

    tq = qbt_ref.shape[4]
    for hd in range(B_HEADS):
        qt = rope_t(qb[:, hd * HEAD_PAD:(hd + 1) * HEAD_PAD].T).astype(BF16)
        for s in range(tm // tq):
            qbt_ref[0, hd, s] = qt[:, s * tq:(s + 1) * tq]

    kr_rot = rope_t(kr.T).T
    for hd in range(B_HEADS):
        kb_ref[0, hd] = (kb[:, hd * HEAD_PAD:(hd + 1) * HEAD_PAD] + kr_rot).astype(BF16)

    vbt = vb.T.astype(BF16)
    nsub = vbt_ref.shape[2]
    tk = vbt_ref.shape[4]
    for hd in range(B_HEADS):
        for s in range(nsub):
            vbt_ref[0, hd, s] = vbt[hd * V_DIM:(hd + 1) * V_DIM, s * tk:(s + 1) * tk]


def _in_proj(x, pos3, invf, g, w1, qn, kvn, wuq, wk, wv, *, tm, tq, tk):
    B, S, _ = x.shape
    nsub = tm // tk
    grid = (B, S // tm)
    full = lambda shape: pl.BlockSpec(shape, lambda b, i: (0,) * len(shape))
    out_shape = (
        jax.ShapeDtypeStruct((B, A_Q_W, S), BF16),
        jax.ShapeDtypeStruct((B, S, A_KV_W), BF16),
        jax.ShapeDtypeStruct((B, A_KV_W, S), BF16),
        jax.ShapeDtypeStruct((B, B_HEADS, S // tq, HEAD_PAD, tq), BF16),
        jax.ShapeDtypeStruct((B, B_HEADS, S, HEAD_PAD), BF16),
        jax.ShapeDtypeStruct((B, B_HEADS, S // tk, V_DIM, tk), BF16),
    )
    out_specs = (
        pl.BlockSpec((1, A_Q_W, tm), lambda b, i: (b, 0, i)),
        pl.BlockSpec((1, tm, A_KV_W), lambda b, i: (b, i, 0)),
        pl.BlockSpec((1, A_KV_W, tm), lambda b, i: (b, 0, i)),
        pl.BlockSpec((1, B_HEADS, tm // tq, HEAD_PAD, tq), lambda b, i: (b, 0, i, 0, 0)),
        pl.BlockSpec((1, B_HEADS, tm, HEAD_PAD), lambda b, i: (b, 0, i, 0)),
        pl.BlockSpec((1, B_HEADS, nsub, V_DIM, tk), lambda b, i: (b, 0, i, 0, 0)),
    )
    in_specs = [
        pl.BlockSpec((1, tm, D_MODEL), lambda b, i: (b, i, 0)),
        pl.BlockSpec((1, 1, tm), lambda b, i: (b, 0, i)),
        full(invf.shape), full(g.shape), full(w1.shape), full(qn.shape), full(kvn.shape),
        full(wuq.shape), full(wk.shape), full(wv.shape),
    ]
    return pl.pallas_call(
        functools.partial(_in_proj_kernel, b_scale=LOG2E / math.sqrt(B_QK_DIM)),
        out_shape=out_shape, grid=grid, in_specs=in_specs, out_specs=out_specs,
        compiler_params=pltpu.CompilerParams(
            dimension_semantics=("parallel", "parallel"),
            vmem_limit_bytes=V7X_VMEM_LIMIT),
        name="in_proj",
    )(x, pos3, invf, g, w1, qn, kvn, wuq, wk, wv)


def _bias_kernel(table_ref, bkt_ref, out_ref):
    bkt = bkt_ref[...]
    row = jax.lax.broadcasted_iota(jnp.int32, bkt.shape, 0)
    group = A_HEADS // A_KV_HEADS
    for hd in range(A_HEADS):
        acc = jnp.full(bkt.shape, -jnp.inf, F32)
        for b in range(N_BUCKETS):
            acc = jnp.where(bkt == b, table_ref[b, hd] * LOG2E, acc)
        g, lane = hd // group, (hd % group) * BLOCK
        out_ref[0, g, :, lane:lane + BLOCK] = acc
        out_ref[1, g, :, lane:lane + BLOCK] = jnp.where(row < BLOCK, -jnp.inf, acc)


def _bucket_table():
    kj = np.arange(2 * BLOCK)[:, None]
    qi = np.arange(BLOCK)[None, :]
    dist = BLOCK + qi - kj
    max_exact = N_BUCKETS // 2
    n = np.maximum(dist, 0)
    nf = np.maximum(n, 1).astype(np.float32)
    large = max_exact + (np.log(nf / max_exact) / math.log(MAX_DISTANCE / max_exact)
                         * (N_BUCKETS - max_exact)).astype(np.int32)
    large = np.minimum(large, N_BUCKETS - 1)
    bkt = np.where(n < max_exact, n, large)
    in_win = (dist >= 0) & (dist < WINDOW)
    return np.where(in_win, bkt, -1).astype(np.int32)


def _rel_bias(table):
    bkt = jnp.asarray(_bucket_table())
    return pl.pallas_call(
        _bias_kernel,
        out_shape=jax.ShapeDtypeStruct(
            (2, A_KV_HEADS, 2 * BLOCK, (A_HEADS // A_KV_HEADS) * BLOCK), F32),
        in_specs=[pl.BlockSpec(memory_space=pltpu.SMEM),
                  pl.BlockSpec(memory_space=pltpu.VMEM)],
        out_specs=pl.BlockSpec(memory_space=pltpu.VMEM),
        name="rel_bias",
    )(table, bkt)


def _swa_kernel(sink_ref, qt_ref, kc_ref, kp_ref, vc_ref, vp_ref, bias_ref, o_ref):
    i = pl.program_id(1)
    ta = qt_ref.shape[2]
    group = A_HEADS // A_KV_HEADS
    width = group * BLOCK
    kcat = jnp.concatenate([kp_ref[0], kc_ref[0]], axis=0)
    vcat = jnp.concatenate([vp_ref[0], vc_ref[0]], axis=1)
    ones = jnp.ones((ONES_ROWS, 2 * BLOCK), BF16)
    zpad = jnp.zeros((A_HEAD_DIM, width), BF16)
    units = [(r, g) for r in range(ta // BLOCK) for g in range(A_KV_HEADS)]

    def scores(r, g):
        kk = kcat[r * BLOCK:(r + 2) * BLOCK]
        qrow = jnp.concatenate(
            [qt_ref[0, hd * A_HEAD_DIM:(hd + 1) * A_HEAD_DIM, r * BLOCK:(r + 1) * BLOCK]
             for hd in range(g * group, (g + 1) * group)], axis=1)
        qpad = jnp.concatenate([qrow, zpad] if g == 0 else [zpad, qrow], axis=0)
        variant = jnp.where(i == 0, 1, 0) if r == 0 else 0
        return _dot(kk, qpad) + bias_ref[variant, g]

    def attend(r, g, s):
        sink = sink_ref[g]
        m = jnp.maximum(jnp.max(s, axis=0, keepdims=True), sink)
        p = jnp.exp2(s - m).astype(BF16)
        vv = vcat[g * A_HEAD_DIM:(g + 1) * A_HEAD_DIM, r * BLOCK:(r + 2) * BLOCK]
        acc = _dot(jnp.concatenate([vv, ones], axis=0), p)
        denom = acc[A_HEAD_DIM:A_HEAD_DIM + 1] + jnp.exp2(sink - m)
        return acc[:A_HEAD_DIM] / denom

    outs = {}
    s_next = scores(*units[0])
    for n, (r, g) in enumerate(units):
        s = s_next
        if n + 1 < len(units):
            s_next = scores(*units[n + 1])
        outs[(r, g)] = attend(r, g, s)
    for r in range(ta // BLOCK):
        heads = [outs[(r, g)][:, hh * BLOCK:(hh + 1) * BLOCK]
                 for g in range(A_KV_HEADS) for hh in range(group)]
        o_ref[0, r * BLOCK:(r + 1) * BLOCK, :] = jnp.concatenate(heads, axis=0).T.astype(BF16)


def _swa(sink_rows, qat, ka, vat, bias, *, ta):
    B, _, S = qat.shape
    sub = ta // BLOCK
    prev = lambda i: jnp.maximum(i * sub - 1, 0)
    return pl.pallas_call(
        _swa_kernel,
        out_shape=jax.ShapeDtypeStruct((B, S, A_Q_W), BF16),
        grid=(B, S // ta),
        in_specs=[
            pl.BlockSpec(sink_rows.shape, lambda b, i: (0, 0, 0)),
            pl.BlockSpec((1, A_Q_W, ta), lambda b, i: (b, 0, i)),
            pl.BlockSpec((1, ta, A_KV_W), lambda b, i: (b, i, 0)),
            pl.BlockSpec((1, BLOCK, A_KV_W), lambda b, i: (b, prev(i), 0)),
            pl.BlockSpec((1, A_KV_W, ta), lambda b, i: (b, 0, i)),
            pl.BlockSpec((1, A_KV_W, BLOCK), lambda b, i: (b, 0, prev(i))),
            pl.BlockSpec(bias.shape, lambda b, i: (0, 0, 0, 0)),
        ],
        out_specs=pl.BlockSpec((1, ta, A_Q_W), lambda b, i: (b, i, 0)),
        compiler_params=pltpu.CompilerParams(
            dimension_semantics=("parallel", "parallel"),
            vmem_limit_bytes=V7X_VMEM_LIMIT),
        name="swa",
    )(sink_rows, qat, ka, ka, vat, vat, bias)


def _mla_kernel(qt_ref, k_ref, vt_ref, o_ref, *scratch):
    n_tiles = qt_ref.shape[2]
    tq = qt_ref.shape[4]
    tk = vt_ref.shape[4]
    nh = qt_ref.shape[1]
    assert tq == 2 * tk
    s_refs = [scratch[2 * hh:2 * hh + 2] for hh in range(nh)]
    p_refs = [scratch[2 * nh + 2 * hh:2 * nh + 2 * hh + 2] for hh in range(nh)]
    acc_refs = scratch[4 * nh:5 * nh]
    cm_ref = scratch[5 * nh]
    ones = jnp.ones((ONES_ROWS, tk), BF16)
    heads = range(nh)

    def qk(hh, blk, par, tile):
        kblk = k_ref[0, hh, pl.ds(pl.multiple_of(blk * tk, tk), tk), :]
        s = _dot(kblk, qt_ref[0, hh, tile])
        s_refs[hh][par][...] = s
        return jnp.max(s, axis=0, keepdims=True)

    def first_scores(tile):
        for hh in heads:
            for par in range(2):
                cm_ref[2 * hh + par, 0:1, :] = qk(hh, par, par, tile)

    def softmax(hh, par, m, cmax):
        m_new = jnp.maximum(m, cmax)
        p_refs[hh][par][...] = jnp.exp2(s_refs[hh][par][...] - m_new).astype(BF16)
        return m_new, jnp.exp2(m - m_new)

    def pv(hh, blk, par, alpha):
        vaug = jnp.concatenate([vt_ref[0, hh, blk], ones], axis=0)
        acc_refs[hh][...] = alpha * acc_refs[hh][...] + _dot(vaug, p_refs[hh][par][...])

    def query_tile(i):
        def pair(jj, carry):
            ms, alphas, cm0, cm1 = carry
            a = 2 * jj
            ms, alphas_a = zip(*[softmax(hh, 0, ms[hh], cm0[hh]) for hh in heads])
            for hh in heads:
                pv(hh, jnp.maximum(a - 1, 0), 1, alphas[hh])
            cm0 = [qk(hh, a + 2, 0, i) for hh in heads]
            ms, alphas_b = zip(*[softmax(hh, 1, ms[hh], cm1[hh]) for hh in heads])
            for hh in heads:
                pv(hh, a, 0, alphas_a[hh])
            cm1 = [qk(hh, a + 3, 1, i) for hh in heads]
            return list(ms), list(alphas_b), cm0, cm1

        def quad(qq, carry):
            return pair(2 * qq + 1, pair(2 * qq, carry))

        for hh in heads:
            p_refs[hh][1][...] = jnp.zeros((tk, tq), BF16)
            acc_refs[hh][...] = jnp.zeros((V_DIM + ONES_ROWS, tq), F32)
        cm0 = [cm_ref[2 * hh, 0:1, :] for hh in heads]
        cm1 = [cm_ref[2 * hh + 1, 0:1, :] for hh in heads]
        ms = [jnp.full((1, tq), -jnp.inf, F32) for _ in heads]
        alphas = [jnp.ones((1, tq), F32) for _ in heads]
        carry = jax.lax.fori_loop(0, i // 2, quad, (ms, alphas, cm0, cm1))
        carry = jax.lax.cond(i % 2 == 1, lambda c: pair(i - 1, c), lambda c: c, carry)
        ms, alphas, cm0, _ = carry

        d0 = 2 * i
        delta = (jax.lax.broadcasted_iota(jnp.int32, (tk, tk), 0)
                 - jax.lax.broadcasted_iota(jnp.int32, (tk, tk), 1))
        thr = i * tq - d0 * tk
        lo, hi = slice(0, tk), slice(tk, tq)

        def masked_softmax(hh, par, cols, m):
            s = jnp.where(delta <= thr, s_refs[hh][par][:, cols], -jnp.inf)
            m_new = jnp.maximum(m, jnp.max(s, axis=0, keepdims=True))
            p_refs[hh][par][:, cols] = jnp.exp2(s - m_new).astype(BF16)
            return m_new, jnp.exp2(m - m_new)

        alphas_a = []
        for hh in heads:
            m_lo, a_lo = masked_softmax(hh, 0, lo, ms[hh][:, lo])
            m_hi = jnp.maximum(ms[hh][:, hi], cm0[hh][:, hi])
            p_refs[hh][0][:, hi] = jnp.exp2(s_refs[hh][0][:, hi] - m_hi).astype(BF16)
            a_hi = jnp.exp2(ms[hh][:, hi] - m_hi)
            ms[hh] = jnp.concatenate([m_lo, m_hi], axis=1)
            alphas_a.append(jnp.concatenate([a_lo, a_hi], axis=1))
        for hh in heads:
            pv(hh, jnp.maximum(d0 - 1, 0), 1, alphas[hh])
        alphas_b = [masked_softmax(hh, 1, hi, ms[hh][:, hi])[1] for hh in heads]
        first_scores(jnp.minimum(i + 1, n_tiles - 1))
        for hh in heads:
            pv(hh, d0, 0, alphas_a[hh])
        for hh in heads:
            vaug = jnp.concatenate([vt_ref[0, hh, d0 + 1], ones], axis=0)
            acc_refs[hh][:, hi] = (alphas_b[hh] * acc_refs[hh][:, hi]
                                   + _dot(vaug, p_refs[hh][1][:, hi]))
        outs = [acc_refs[hh][:V_DIM, :] / acc_refs[hh][V_DIM:V_DIM + 1, :] for hh in heads]
        rows = pl.ds(pl.multiple_of(i * tq, tq), tq)
        o_ref[0, rows, :] = jnp.concatenate(outs, axis=0).T.astype(BF16)

    first_scores(0)

    def tile_step(i, carry):
        query_tile(i)
        return carry

    jax.lax.fori_loop(0, n_tiles, tile_step, 0)


def _mla(qbt, kb, vbt, *, nh=2):
    B, H, n_tiles, _, tq = qbt.shape
    S = n_tiles * tq
    tk = vbt.shape[4]
    return pl.pallas_call(
        _mla_kernel,
        out_shape=jax.ShapeDtypeStruct((B, S, H * V_DIM), BF16),
        grid=(B, H // nh),
        in_specs=[
            pl.BlockSpec((1, nh, n_tiles, HEAD_PAD, tq), lambda b, hp: (b, hp, 0, 0, 0)),
            pl.BlockSpec((1, nh, S, HEAD_PAD), lambda b, hp: (b, hp, 0, 0)),
            pl.BlockSpec((1, nh, S // tk, V_DIM, tk), lambda b, hp: (b, hp, 0, 0, 0)),
        ],
        out_specs=pl.BlockSpec((1, S, nh * V_DIM), lambda b, hp: (b, 0, hp)),
        scratch_shapes=([pltpu.VMEM((tk, tq), F32)] * (2 * nh)
                        + [pltpu.VMEM((tk, tq), BF16)] * (2 * nh)
                        + [pltpu.VMEM((V_DIM + ONES_ROWS, tq), F32)] * nh
                        + [pltpu.VMEM((2 * nh, 8, tq), F32)]),
        compiler_params=pltpu.CompilerParams(
            dimension_semantics=("parallel", "parallel"),
            vmem_limit_bytes=V7X_VMEM_LIMIT),
        name="mla",
    )(qbt, kb, vbt)


def _merge_kernel(x_ref, oa_ref, ob_ref, g_ref, wg_ref, wa_ref, wb_ref, wo_ref, o_ref):
    x = x_ref[0]
    h = _rms(x, g_ref[...]).astype(BF16)
    gates = jax.nn.sigmoid(_dot(h, wg_ref[...]))
    ya = _dot(oa_ref[0], wa_ref[...])
    yb = _dot(ob_ref[0], wb_ref[...])
    merged = gates[:, :D_MODEL] * ya + gates[:, D_MODEL:] * yb
    o_ref[0] = x + _dot(merged.astype(BF16), wo_ref[...])


def _merge(x, oa, ob, g, wg, wa, wb, wo, *, tm):
    B, S, _ = x.shape
    full = lambda shape: pl.BlockSpec(shape, lambda b, i: (0,) * len(shape))
    tok = lambda w: pl.BlockSpec((1, tm, w), lambda b, i: (b, i, 0))
    return pl.pallas_call(
        _merge_kernel,
        out_shape=jax.ShapeDtypeStruct(x.shape, F32),
        grid=(B, S // tm),
        in_specs=[tok(D_MODEL), tok(A_Q_W), tok(B_HEADS * V_DIM), full(g.shape),
                  full(wg.shape), full(wa.shape), full(wb.shape), full(wo.shape)],
        out_specs=tok(D_MODEL),
        compiler_params=pltpu.CompilerParams(
            dimension_semantics=("parallel", "parallel"),
            vmem_limit_bytes=V7X_VMEM_LIMIT),
        name="merge",
    )(x, oa, ob, g, wg, wa, wb, wo)


def _mlp_kernel(x_ref, g_ref, w1_ref, w2_ref, gf_ref, o_ref, *, final):
    x = x_ref[0]
    h = _rms(x, g_ref[...]).astype(BF16)
    u = jnp.square(jnp.maximum(_dot(h, w1_ref[...]), 0.0)).astype(BF16)
    y = x + _dot(u, w2_ref[...])
    if final:
        y = _rms(y, gf_ref[...])
    o_ref[0] = y


def _mlp(x, g, w1, w2, gf, *, tm, final):
    B, S, _ = x.shape
    full = lambda shape: pl.BlockSpec(shape, lambda b, i: (0,) * len(shape))
    tok = pl.BlockSpec((1, tm, D_MODEL), lambda b, i: (b, i, 0))
    return pl.pallas_call(
        functools.partial(_mlp_kernel, final=final),
        out_shape=jax.ShapeDtypeStruct(x.shape, F32),
        grid=(B, S // tm),
        in_specs=[tok, full(g.shape), full(w1.shape), full(w2.shape), full(gf.shape)],
        out_specs=tok,
        compiler_params=pltpu.CompilerParams(
            dimension_semantics=("parallel", "parallel"),
            vmem_limit_bytes=V7X_VMEM_LIMIT),
        name="mlp_final" if final else "mlp",
    )(x, g, w1, w2, gf)


def _prep_layer(w_in, w_uq, w_ukv):
    o = 0
    wqa = w_in[:, o:o + A_Q_W]; o += A_Q_W
    wka = w_in[:, o:o + A_KV_W]; o += A_KV_W
    wva = w_in[:, o:o + A_KV_W]; o += A_KV_W
    wcq = w_in[:, o:o + Q_LORA]; o += Q_LORA
    wckv = w_in[:, o:o + KV_LORA]; o += KV_LORA
    wkr = w_in[:, o:o + ROPE_DIM]; o += ROPE_DIM
    wg = w_in[:, o:]
    wkr_pad = jnp.pad(wkr, ((0, 0), (NOPE_DIM, HEAD_PAD - NOPE_DIM - ROPE_DIM)))
    w1 = jnp.concatenate([wqa, wka, wva, wcq, wckv, wkr_pad], axis=1).astype(BF16)
    wuq = w_uq.reshape(Q_LORA, B_HEADS, B_QK_DIM)
    wuq = jnp.pad(wuq, ((0, 0), (0, 0), (0, HEAD_PAD - B_QK_DIM)))
    wuq = wuq.reshape(Q_LORA, B_HEADS * HEAD_PAD).astype(BF16)
    wukv = w_ukv.reshape(KV_LORA, B_HEADS, NOPE_DIM + V_DIM)
    wk = jnp.pad(wukv[:, :, :NOPE_DIM], ((0, 0), (0, 0), (0, HEAD_PAD - NOPE_DIM)))
    wk = wk.reshape(KV_LORA, B_HEADS * HEAD_PAD).astype(BF16)
    wv = wukv[:, :, NOPE_DIM:].reshape(KV_LORA, B_HEADS * V_DIM).astype(BF16)
    return w1, wg.astype(BF16), wuq, wk, wv


def kernel(x, positions, rel_bias_table, norm_mix, w_in, attn_sinks, q_norm, kv_norm,
           w_uq, w_ukv, w_branch_a, w_branch_b, w_out, norm_mlp, w_ff1, w_ff2, norm_final):
    B, S, _ = x.shape
    depth = w_in.shape[0]
    tm = min(TOK_TILE, S)
    t = min(MLA_TILE, S)
    tk = MLA_KEYS
    ta = min(SWA_TILE, S)
    assert S % tm == 0 and S % ta == 0 and tm % t == 0 and t == 2 * tk and ta % BLOCK == 0

    half = ROPE_DIM // 2
    inv_freq = ROPE_THETA ** (-jnp.arange(half, dtype=F32) / half)
    invf = inv_freq.reshape(half, 1)
    pos3 = positions.reshape(B, 1, S)
    bias = _rel_bias(rel_bias_table.astype(F32))
    row = lambda v: v.reshape(1, -1).astype(F32)

    for l in range(depth):
        w1, wg, wuq, wk, wv = _prep_layer(w_in[l], w_uq[l], w_ukv[l])
        qat, ka, vat, qbt, kb, vbt = _in_proj(
            x, pos3, invf, row(norm_mix[l]), w1, row(q_norm[l]), row(kv_norm[l]),
            wuq, wk, wv, tm=tm, tq=t, tk=tk)
        sink_rows = jnp.repeat(attn_sinks[l].astype(F32).reshape(A_KV_HEADS, 1, -1) * LOG2E,
                               BLOCK, axis=2)
        oa = _swa(sink_rows, qat, ka, vat, bias, ta=ta)
        ob = _mla(qbt, kb, vbt)
        x = _merge(x, oa, ob, row(norm_mix[l]), wg, w_branch_a[l].astype(BF16),
                   w_branch_b[l].astype(BF16), w_out[l].astype(BF16), tm=tm)
        x = _mlp(x, row(norm_mlp[l]), w_ff1[l].astype(BF16), w_ff2[l].astype(BF16),
                 row(norm_final), tm=tm, final=(l == depth - 1))
    return x
```

```python
import functools
import math

import jax
import jax.numpy as jnp
import numpy as np
from jax.experimental import pallas as pl
from jax.experimental.pallas import tpu as pltpu

D_MODEL = 1024
A_HEADS = 8
A_KV_HEADS = 2
A_HEAD_DIM = 64
WINDOW = 128
BLOCK = 128
B_HEADS = 8
Q_LORA = 384
KV_LORA = 256
NOPE_DIM = 64
ROPE_DIM = 32
V_DIM = 64
ROPE_THETA = 10000.0
N_BUCKETS = 32
MAX_DISTANCE = 128
D_FF = 4 * D_MODEL
EPS = 1e-5
A_Q_W = A_HEADS * A_HEAD_DIM
A_KV_W = A_KV_HEADS * A_HEAD_DIM
B_QK_DIM = NOPE_DIM + ROPE_DIM

LANES = 128
V7X_VMEM_LIMIT = 56 * 1024 * 1024

HEAD_PAD = LANES
ONES_ROWS = 16
TOK_TILE = 512
MLA_TILE = 512
MLA_KEYS = 256
SWA_TILE = 1024

BF16 = jnp.bfloat16
F32 = jnp.float32
LOG2E = math.log2(math.e)


def _rms(x, g):
    return x * jax.lax.rsqrt(jnp.mean(x * x, axis=-1, keepdims=True) + EPS) * g


def _dot(a, b):
    return jnp.dot(a, b, preferred_element_type=F32)


def _in_proj_kernel(x_ref, pos_ref, invf_ref, g_ref, w1_ref, qn_ref, kvn_ref,
                    wuq_ref, wk_ref, wv_ref,
                    qat_ref, ka_ref, vat_ref, qbt_ref, kb_ref, vbt_ref, *, b_scale):
    tm = x_ref.shape[1]
    h = _rms(x_ref[0], g_ref[...]).astype(BF16)
    swa_w = A_Q_W + 2 * A_KV_W
    zl = _dot(h, w1_ref[:, swa_w:])
    za = _dot(h, w1_ref[:, :swa_w])
    cq = zl[:, :Q_LORA]
    ckv = zl[:, Q_LORA:Q_LORA + KV_LORA]
    kr = zl[:, Q_LORA + KV_LORA:]
    qa = za[:, :A_Q_W]
    ka = za[:, A_Q_W:A_Q_W + A_KV_W]
    va = za[:, A_Q_W + A_KV_W:]

    qat_ref[0] = (qa * (LOG2E / math.sqrt(A_HEAD_DIM))).T.astype(BF16)
    ka_ref[0] = ka.astype(BF16)
    vat_ref[0] = va.T.astype(BF16)

    cqn = _rms(cq, qn_ref[...]).astype(BF16)
    qb = _dot(cqn, wuq_ref[...]) * b_scale
    ckvn = _rms(ckv, kvn_ref[...]).astype(BF16)
    kb = _dot(ckvn, wk_ref[...])
    vb = _dot(ckvn, wv_ref[...])

    ang = pos_ref[0].astype(F32) * invf_ref[...]
    cos, sin = jnp.cos(ang), jnp.sin(ang)
    half = ROPE_DIM // 2
    r0, r1, r2 = NOPE_DIM, NOPE_DIM + half, NOPE_DIM + ROPE_DIM

    def rope_t(t):
        t1, t2 = t[r0:r1], t[r1:r2]
        return jnp.concatenate(
            [t[:r0], t1 * cos - t2 * sin, t2 * cos + t1 * sin, t[r2:]], axis=0)

    tq = qbt_ref.shape[4]
    for hd in range(B_HEADS):
        qt = rope_t(qb[:, hd * HEAD_PAD:(hd + 1) * HEAD_PAD].T).astype(BF16)
        for s in range(tm // tq):
            qbt_ref[0, hd, s] = qt[:, s * tq:(s + 1) * tq]

    kr_rot = rope_t(kr.T).T
    for hd in range(B_HEADS):
        kb_ref[0, hd] = (kb[:, hd * HEAD_PAD:(hd + 1) * HEAD_PAD] + kr_rot).astype(BF16)

    vbt = vb.T.astype(BF16)
    nsub = vbt_ref.shape[2]
    tk = vbt_ref.shape[4]
    for hd in range(B_HEADS):
        for s in range(nsub):
            vbt_ref[0, hd, s] = vbt[hd * V_DIM:(hd + 1) * V_DIM, s * tk:(s + 1) * tk]


def _in_proj(x, pos3, invf, g, w1, qn, kvn, wuq, wk, wv, *, tm, tq, tk):
    B, S, _ = x.shape
    nsub = tm // tk
    grid = (B, S // tm)
    full = lambda shape: pl.BlockSpec(shape, lambda b, i: (0,) * len(shape))
    out_shape = (
        jax.ShapeDtypeStruct((B, A_Q_W, S), BF16),
        jax.ShapeDtypeStruct((B, S, A_KV_W), BF16),
        jax.ShapeDtypeStruct((B, A_KV_W, S), BF16),
        jax.ShapeDtypeStruct((B, B_HEADS, S // tq, HEAD_PAD, tq), BF16),
        jax.ShapeDtypeStruct((B, B_HEADS, S, HEAD_PAD), BF16),
        jax.ShapeDtypeStruct((B, B_HEADS, S // tk, V_DIM, tk), BF16),
    )
    out_specs = (
        pl.BlockSpec((1, A_Q_W, tm), lambda b, i: (b, 0, i)),
        pl.BlockSpec((1, tm, A_KV_W), lambda b, i: (b, i, 0)),
        pl.BlockSpec((1, A_KV_W, tm), lambda b, i: (b, 0, i)),
        pl.BlockSpec((1, B_HEADS, tm // tq, HEAD_PAD, tq), lambda b, i: (b, 0, i, 0, 0)),
        pl.BlockSpec((1, B_HEADS, tm, HEAD_PAD), lambda b, i: (b, 0, i, 0)),
        pl.BlockSpec((1, B_HEADS, nsub, V_DIM, tk), lambda b, i: (b, 0, i, 0, 0)),
    )
    in_specs = [
        pl.BlockSpec((1, tm, D_MODEL), lambda b, i: (b, i, 0)),
        pl.BlockSpec((1, 1, tm), lambda b, i: (b, 0, i)),
        full(invf.shape), full(g.shape), full(w1.shape), full(qn.shape), full(kvn.shape),
        full(wuq.shape), full(wk.shape), full(wv.shape),
    ]
    return pl.pallas_call(
        functools.partial(_in_proj_kernel, b_scale=LOG2E / math.sqrt(B_QK_DIM)),
        out_shape=out_shape, grid=grid, in_specs=in_specs, out_specs=out_specs,
        compiler_params=pltpu.CompilerParams(
            dimension_semantics=("parallel", "parallel"),
            vmem_limit_bytes=V7X_VMEM_LIMIT),
        name="in_proj",
    )(x, pos3, invf, g, w1, qn, kvn, wuq, wk, wv)


def _bias_kernel(table_ref, bkt_ref, out_ref):
    bkt = bkt_ref[...]
    row = jax.lax.broadcasted_iota(jnp.int32, bkt.shape, 0)
    group = A_HEADS // A_KV_HEADS
    for hd in range(A_HEADS):
        acc = jnp.full(bkt.shape, -jnp.inf, F32)
        for b in range(N_BUCKETS):
            acc = jnp.where(bkt == b, table_ref[b, hd] * LOG2E, acc)
        g, lane = hd // group, (hd % group) * BLOCK
        out_ref[0, g, :, lane:lane + BLOCK] = acc
        out_ref[1, g, :, lane:lane + BLOCK] = jnp.where(row < BLOCK, -jnp.inf, acc)


def _bucket_table():
    kj = np.arange(2 * BLOCK)[:, None]
    qi = np.arange(BLOCK)[None, :]
    dist = BLOCK + qi - kj
    max_exact = N_BUCKETS // 2
    n = np.maximum(dist, 0)
    nf = np.maximum(n, 1).astype(np.float32)
    large = max_exact + (np.log(nf / max_exact) / math.log(MAX_DISTANCE / max_exact)
                         * (N_BUCKETS - max_exact)).astype(np.int32)
    large = np.minimum(large, N_BUCKETS - 1)
    bkt = np.where(n < max_exact, n, large)
    in_win = (dist >= 0) & (dist < WINDOW)
    return np.where(in_win, bkt, -1).astype(np.int32)


def _rel_bias(table):
    bkt = jnp.asarray(_bucket_table())
    return pl.pallas_call(
        _bias_kernel,
        out_shape=jax.ShapeDtypeStruct(
            (2, A_KV_HEADS, 2 * BLOCK, (A_HEADS // A_KV_HEADS) * BLOCK), F32),
        in_specs=[pl.BlockSpec(memory_space=pltpu.SMEM),
                  pl.BlockSpec(memory_space=pltpu.VMEM)],
        out_specs=pl.BlockSpec(memory_space=pltpu.VMEM),
        name="rel_bias",
    )(table, bkt)


def _swa_kernel(sink_ref, qt_ref, kc_ref, kp_ref, vc_ref, vp_ref, bias_ref, o_ref):
    i = pl.program_id(1)
    ta = qt_ref.shape[2]
    group = A_HEADS // A_KV_HEADS
    width = group * BLOCK
    kcat = jnp.concatenate([kp_ref[0], kc_ref[0]], axis=0)
    vcat = jnp.concatenate([vp_ref[0], vc_ref[0]], axis=1)
    ones = jnp.ones((ONES_ROWS, 2 * BLOCK), BF16)
    zpad = jnp.zeros((A_HEAD_DIM, width), BF16)
    units = [(r, g) for r in range(ta // BLOCK) for g in range(A_KV_HEADS)]

    def scores(r, g):
        kk = kcat[r * BLOCK:(r + 2) * BLOCK]
        qrow = jnp.concatenate(
            [qt_ref[0, hd * A_HEAD_DIM:(hd + 1) * A_HEAD_DIM, r * BLOCK:(r + 1) * BLOCK]
             for hd in range(g * group, (g + 1) * group)], axis=1)
        qpad = jnp.concatenate([qrow, zpad] if g == 0 else [zpad, qrow], axis=0)
        variant = jnp.where(i == 0, 1, 0) if r == 0 else 0
        return _dot(kk, qpad) + bias_ref[variant, g]

    def attend(r, g, s):
        sink = sink_ref[g]
        m = jnp.maximum(jnp.max(s, axis=0, keepdims=True), sink)
        p = jnp.exp2(s - m).astype(BF16)
        vv = vcat[g * A_HEAD_DIM:(g + 1) * A_HEAD_DIM, r * BLOCK:(r + 2) * BLOCK]
        acc = _dot(jnp.concatenate([vv, ones], axis=0), p)
        denom = acc[A_HEAD_DIM:A_HEAD_DIM + 1] + jnp.exp2(sink - m)
        return acc[:A_HEAD_DIM] / denom

    outs = {}
    pending = [scores(*units[0]), scores(*units[1])]
    for n, (r, g) in enumerate(units):
        s = pending.pop(0)
        if n + 2 < len(units):
            pending.append(scores(*units[n + 2]))
        outs[(r, g)] = attend(r, g, s)
    for r in range(ta // BLOCK):
        heads = [outs[(r, g)][:, hh * BLOCK:(hh + 1) * BLOCK]
                 for g in range(A_KV_HEADS) for hh in range(group)]
        o_ref[0, r * BLOCK:(r + 1) * BLOCK, :] = jnp.concatenate(heads, axis=0).T.astype(BF16)


def _swa(sink_rows, qat, ka, vat, bias, *, ta):
    B, _, S = qat.shape
    sub = ta // BLOCK
    prev = lambda i: jnp.maximum(i * sub - 1, 0)
    return pl.pallas_call(
        _swa_kernel,
        out_shape=jax.ShapeDtypeStruct((B, S, A_Q_W), BF16),
        grid=(B, S // ta),
        in_specs=[
            pl.BlockSpec(sink_rows.shape, lambda b, i: (0, 0, 0)),
            pl.BlockSpec((1, A_Q_W, ta), lambda b, i: (b, 0, i)),
            pl.BlockSpec((1, ta, A_KV_W), lambda b, i: (b, i, 0)),
            pl.BlockSpec((1, BLOCK, A_KV_W), lambda b, i: (b, prev(i), 0)),
            pl.BlockSpec((1, A_KV_W, ta), lambda b, i: (b, 0, i)),
            pl.BlockSpec((1, A_KV_W, BLOCK), lambda b, i: (b, 0, prev(i))),
            pl.BlockSpec(bias.shape, lambda b, i: (0, 0, 0, 0)),
        ],
        out_specs=pl.BlockSpec((1, ta, A_Q_W), lambda b, i: (b, i, 0)),
        compiler_params=pltpu.CompilerParams(
            dimension_semantics=("parallel", "parallel"),
            vmem_limit_bytes=V7X_VMEM_LIMIT),
        name="swa",
    )(sink_rows, qat, ka, ka, vat, vat, bias)


def _mla_kernel(qt_ref, k_ref, vt_ref, o_ref, *scratch):
    n_tiles = qt_ref.shape[2]
    tq = qt_ref.shape[4]
    tk = vt_ref.shape[4]
    nh = qt_ref.shape[1]
    assert tq == 2 * tk
    s_refs = [scratch[2 * hh:2 * hh + 2] for hh in range(nh)]
    p_refs = [scratch[2 * nh + 2 * hh:2 * nh + 2 * hh + 2] for hh in range(nh)]
    acc_refs = scratch[4 * nh:5 * nh]
    cm_ref = scratch[5 * nh]
    ones = jnp.ones((ONES_ROWS, tk), BF16)
    heads = range(nh)

    def qk(hh, blk, par, tile):
        kblk = k_ref[0, hh, pl.ds(pl.multiple_of(blk * tk, tk), tk), :]
        s = _dot(kblk, qt_ref[0, hh, tile])
        s_refs[hh][par][...] = s
        return jnp.max(s, axis=0, keepdims=True)

    def first_scores(tile):
        for hh in heads:
            for par in range(2):
                cm_ref[2 * hh + par, 0:1, :] = qk(hh, par, par, tile)

    def softmax(hh, par, m, cmax):
        m_new = jnp.maximum(m, cmax)
        p_refs[hh][par][...] = jnp.exp2(s_refs[hh][par][...] - m_new).astype(BF16)
        return m_new, jnp.exp2(m - m_new)

    def pv(hh, blk, par, alpha):
        vaug = jnp.concatenate([vt_ref[0, hh, blk], ones], axis=0)
        acc_refs[hh][...] = alpha * acc_refs[hh][...] + _dot(vaug, p_refs[hh][par][...])

    def query_tile(i):
        def pair(jj, carry):
            ms, alphas, cm0, cm1 = carry
            a = 2 * jj
            ms, alphas_a = zip(*[softmax(hh, 0, ms[hh], cm0[hh]) for hh in heads])
            for hh in heads:
                pv(hh, jnp.maximum(a - 1, 0), 1, alphas[hh])
            cm0 = [qk(hh, a + 2, 0, i) for hh in heads]
            ms, alphas_b = zip(*[softmax(hh, 1, ms[hh], cm1[hh]) for hh in heads])
            for hh in heads:
                pv(hh, a, 0, alphas_a[hh])
            cm1 = [qk(hh, a + 3, 1, i) for hh in heads]
            return list(ms), list(alphas_b), cm0, cm1

        def quad(qq, carry):
            return pair(2 * qq + 1, pair(2 * qq, carry))

        for hh in heads:
            p_refs[hh][1][...] = jnp.zeros((tk, tq), BF16)
            acc_refs[hh][...] = jnp.zeros((V_DIM + ONES_ROWS, tq), F32)
        cm0 = [cm_ref[2 * hh, 0:1, :] for hh in heads]
        cm1 = [cm_ref[2 * hh + 1, 0:1, :] for hh in heads]
        ms = [jnp.full((1, tq), -jnp.inf, F32) for _ in heads]
        alphas = [jnp.ones((1, tq), F32) for _ in heads]
        carry = jax.lax.fori_loop(0, i // 2, quad, (ms, alphas, cm0, cm1))
        carry = jax.lax.cond(i % 2 == 1, lambda c: pair(i - 1, c), lambda c: c, carry)
        ms, alphas, cm0, _ = carry

        d0 = 2 * i
        delta = (jax.lax.broadcasted_iota(jnp.int32, (tk, tk), 0)
                 - jax.lax.broadcasted_iota(jnp.int32, (tk, tk), 1))
        thr = i * tq - d0 * tk
        lo, hi = slice(0, tk), slice(tk, tq)

        def masked_softmax(hh, par, cols, m):
            s = jnp.where(delta <= thr, s_refs[hh][par][:, cols], -jnp.inf)
            m_new = jnp.maximum(m, jnp.max(s, axis=0, keepdims=True))
            p_refs[hh][par][:, cols] = jnp.exp2(s - m_new).astype(BF16)
            return m_new, jnp.exp2(m - m_new)

        alphas_a = []
        for hh in heads:
            m_lo, a_lo = masked_softmax(hh, 0, lo, ms[hh][:, lo])
            m_hi = jnp.maximum(ms[hh][:, hi], cm0[hh][:, hi])
            p_refs[hh][0][:, hi] = jnp.exp2(s_refs[hh][0][:, hi] - m_hi).astype(BF16)
            a_hi = jnp.exp2(ms[hh][:, hi] - m_hi)
            ms[hh] = jnp.concatenate([m_lo, m_hi], axis=1)
            alphas_a.append(jnp.concatenate([a_lo, a_hi], axis=1))
        for hh in heads:
            pv(hh, jnp.maximum(d0 - 1, 0), 1, alphas[hh])
        alphas_b = [masked_softmax(hh, 1, hi, ms[hh][:, hi])[1] for hh in heads]
        first_scores(jnp.minimum(i + 1, n_tiles - 1))
        for hh in heads:
            pv(hh, d0, 0, alphas_a[hh])
        for hh in heads:
            vaug = jnp.concatenate([vt_ref[0, hh, d0 + 1], ones], axis=0)
            acc_refs[hh][:, hi] = (alphas_b[hh] * acc_refs[hh][:, hi]
                                   + _dot(vaug, p_refs[hh][1][:, hi]))
        outs = [acc_refs[hh][:V_DIM, :] / acc_refs[hh][V_DIM:V_DIM + 1, :] for hh in heads]
        rows = pl.ds(pl.multiple_of(i * tq, tq), tq)
        o_ref[0, rows, :] = jnp.concatenate(outs, axis=0).T.astype(BF16)

    first_scores(0)

    def tile_step(i, carry):
        query_tile(i)
        return carry

    jax.lax.fori_loop(0, n_tiles, tile_step, 0)


def _mla(qbt, kb, vbt, *, nh=2):
    B, H, n_tiles, _, tq = qbt.shape
    S = n_tiles * tq
    tk = vbt.shape[4]
    return pl.pallas_call(
        _mla_kernel,
        out_shape=jax.ShapeDtypeStruct((B, S, H * V_DIM), BF16),
        grid=(B, H // nh),
        in_specs=[
            pl.BlockSpec((1, nh, n_tiles, HEAD_PAD, tq), lambda b, hp: (b, hp, 0, 0, 0)),
            pl.BlockSpec((1, nh, S, HEAD_PAD), lambda b, hp: (b, hp, 0, 0)),
            pl.BlockSpec((1, nh, S // tk, V_DIM, tk), lambda b, hp: (b, hp, 0, 0, 0)),
        ],
        out_specs=pl.BlockSpec((1, S, nh * V_DIM), lambda b, hp: (b, 0, hp)),
        scratch_shapes=([pltpu.VMEM((tk, tq), F32)] * (2 * nh)
                        + [pltpu.VMEM((tk, tq), BF16)] * (2 * nh)
                        + [pltpu.VMEM((V_DIM + ONES_ROWS, tq), F32)] * nh
                        + [pltpu.VMEM((2 * nh, 8, tq), F32)]),
        compiler_params=pltpu.CompilerParams(
            dimension_semantics=("parallel", "parallel"),
            vmem_limit_bytes=V7X_VMEM_LIMIT),
        name="mla",
    )(qbt, kb, vbt)


def _merge_kernel(x_ref, oa_ref, ob_ref, g_ref, wg_ref, wa_ref, wb_ref, wo_ref, o_ref):
    x = x_ref[0]
    h = _rms(x, g_ref[...]).astype(BF16)
    gates = jax.nn.sigmoid(_dot(h, wg_ref[...]))
    ya = _dot(oa_ref[0], wa_ref[...])
    yb = _dot(ob_ref[0], wb_ref[...])
    merged = gates[:, :D_MODEL] * ya + gates[:, D_MODEL:] * yb
    o_ref[0] = x + _dot(merged.astype(BF16), wo_ref[...])


def _merge(x, oa, ob, g, wg, wa, wb, wo, *, tm):
    B, S, _ = x.shape
    full = lambda shape: pl.BlockSpec(shape, lambda b, i: (0,) * len(shape))
    tok = lambda w: pl.BlockSpec((1, tm, w), lambda b, i: (b, i, 0))
    return pl.pallas_call(
        _merge_kernel,
        out_shape=jax.ShapeDtypeStruct(x.shape, F32),
        grid=(B, S // tm),
        in_specs=[tok(D_MODEL), tok(A_Q_W), tok(B_HEADS * V_DIM), full(g.shape),
                  full(wg.shape), full(wa.shape), full(wb.shape), full(wo.shape)],
        out_specs=tok(D_MODEL),
        compiler_params=pltpu.CompilerParams(
            dimension_semantics=("parallel", "parallel"),
            vmem_limit_bytes=V7X_VMEM_LIMIT),
        name="merge",
    )(x, oa, ob, g, wg, wa, wb, wo)


def _mlp_kernel(x_ref, g_ref, w1_ref, w2_ref, gf_ref, o_ref, *, final):
    x = x_ref[0]
    h = _rms(x, g_ref[...]).astype(BF16)
    u = jnp.square(jnp.maximum(_dot(h, w1_ref[...]), 0.0)).astype(BF16)
    y = x + _dot(u, w2_ref[...])
    if final:
        y = _rms(y, gf_ref[...])
    o_ref[0] = y


def _mlp(x, g, w1, w2, gf, *, tm, final):
    B, S, _ = x.shape
    full = lambda shape: pl.BlockSpec(shape, lambda b, i: (0,) * len(shape))
    tok = pl.BlockSpec((1, tm, D_MODEL), lambda b, i: (b, i, 0))
    return pl.pallas_call(
        functools.partial(_mlp_kernel, final=final),
        out_shape=jax.ShapeDtypeStruct(x.shape, F32),
        grid=(B, S // tm),
        in_specs=[tok, full(g.shape), full(w1.shape), full(w2.shape), full(gf.shape)],
        out_specs=tok,
        compiler_params=pltpu.CompilerParams(
            dimension_semantics=("parallel", "parallel"),
            vmem_limit_bytes=V7X_VMEM_LIMIT),
        name="mlp_final" if final else "mlp",
    )(x, g, w1, w2, gf)


def _prep_layer(w_in, w_uq, w_ukv):
    o = 0
    wqa = w_in[:, o:o + A_Q_W]; o += A_Q_W
    wka = w_in[:, o:o + A_KV_W]; o += A_KV_W
    wva = w_in[:, o:o + A_KV_W]; o += A_KV_W
    wcq = w_in[:, o:o + Q_LORA]; o += Q_LORA
    wckv = w_in[:, o:o + KV_LORA]; o += KV_LORA
    wkr = w_in[:, o:o + ROPE_DIM]; o += ROPE_DIM
    wg = w_in[:, o:]
    wkr_pad = jnp.pad(wkr, ((0, 0), (NOPE_DIM, HEAD_PAD - NOPE_DIM - ROPE_DIM)))
    w1 = jnp.concatenate([wqa, wka, wva, wcq, wckv, wkr_pad], axis=1).astype(BF16)
    wuq = w_uq.reshape(Q_LORA, B_HEADS, B_QK_DIM)
    wuq = jnp.pad(wuq, ((0, 0), (0, 0), (0, HEAD_PAD - B_QK_DIM)))
    wuq = wuq.reshape(Q_LORA, B_HEADS * HEAD_PAD).astype(BF16)
    wukv = w_ukv.reshape(KV_LORA, B_HEADS, NOPE_DIM + V_DIM)
    wk = jnp.pad(wukv[:, :, :NOPE_DIM], ((0, 0), (0, 0), (0, HEAD_PAD - NOPE_DIM)))
    wk = wk.reshape(KV_LORA, B_HEADS * HEAD_PAD).astype(BF16)
    wv = wukv[:, :, NOPE_DIM:].reshape(KV_LORA, B_HEADS * V_DIM).astype(BF16)
    return w1, wg.astype(BF16), wuq, wk, wv


def kernel(x, positions, rel_bias_table, norm_mix, w_in, attn_sinks, q_norm, kv_norm,
           w_uq, w_ukv, w_branch_a, w_branch_b, w_out, norm_mlp, w_ff1, w_ff2, norm_final):
    B, S, _ = x.shape
    depth = w_in.shape[0]
    tm = min(TOK_TILE, S)
    t = min(MLA_TILE, S)
    tk = MLA_KEYS
    ta = min(SWA_TILE, S)
    assert S % tm == 0 and S % ta == 0 and tm % t == 0 and t == 2 * tk and ta % BLOCK == 0

    half = ROPE_DIM // 2
    inv_freq = ROPE_THETA ** (-jnp.arange(half, dtype=F32) / half)
    invf = inv_freq.reshape(half, 1)
    pos3 = positions.reshape(B, 1, S)
    bias = _rel_bias(rel_bias_table.astype(F32))
    row = lambda v: v.reshape(1, -1).astype(F32)

    for l in range(depth):
        w1, wg, wuq, wk, wv = _prep_layer(w_in[l], w_uq[l], w_ukv[l])
        qat, ka, vat, qbt, kb, vbt = _in_proj(
            x, pos3, invf, row(norm_mix[l]), w1, row(q_norm[l]), row(kv_norm[l]),
            wuq, wk, wv, tm=tm, tq=t, tk=tk)
        sink_rows = jnp.repeat(attn_sinks[l].astype(F32).reshape(A_KV_HEADS, 1, -1) * LOG2E,
                               BLOCK, axis=2)
        oa = _swa(sink_rows, qat, ka, vat, bias, ta=ta)
        ob = _mla(qbt, kb, vbt)
        x = _merge(x, oa, ob, row(norm_mix[l]), wg, w_branch_a[l].astype(BF16),
                   w_branch_b[l].astype(BF16), w_out[l].astype(BF16), tm=tm)
        x = _mlp(x, row(norm_mlp[l]), w_ff1[l].astype(BF16), w_ff2[l].astype(BF16),
                 row(norm_final), tm=tm, final=(l == depth - 1))
    return x
```

```python
import functools
import math

import jax
import jax.numpy as jnp
import numpy as np
from jax.experimental import pallas as pl
from jax.experimental.pallas import tpu as pltpu

D_MODEL = 1024
A_HEADS = 8
A_KV_HEADS = 2
A_HEAD_DIM = 64
WINDOW = 128
BLOCK = 128
B_HEADS = 8
Q_LORA = 384
KV_LORA = 256
NOPE_DIM = 64
ROPE_DIM = 32
V_DIM = 64
ROPE_THETA = 10000.0
N_BUCKETS = 32
MAX_DISTANCE = 128
D_FF = 4 * D_MODEL
EPS = 1e-5
A_Q_W = A_HEADS * A_HEAD_DIM
A_KV_W = A_KV_HEADS * A_HEAD_DIM
B_QK_DIM = NOPE_DIM + ROPE_DIM

LANES = 128
V7X_VMEM_LIMIT = 56 * 1024 * 1024

HEAD_PAD = LANES
ONES_ROWS = 16
TOK_TILE = 512
MLA_TILE = 512
MLA_KEYS = 256
SWA_TILE = 2048
MERGE_TILE = 1024

BF16 = jnp.bfloat16
F32 = jnp.float32
LOG2E = math.log2(math.e)


def _rms(x, g):
    return x * jax.lax.rsqrt(jnp.mean(x * x, axis=-1, keepdims=True) + EPS) * g


def _dot(a, b):
    return jnp.dot(a, b, preferred_element_type=F32)


def _in_proj_kernel(x_ref, pos_ref, invf_ref, g_ref, w1_ref, qn_ref, kvn_ref,
                    wuq_ref, wk_ref, wv_ref,
                    qat_ref, ka_ref, vat_ref, qbt_ref, kb_ref, vbt_ref, *, b_scale):
    tm = x_ref.shape[1]
    h = _rms(x_ref[0], g_ref[...]).astype(BF16)
    swa_w = A_Q_W + 2 * A_KV_W
    zl = _dot(h, w1_ref[:, swa_w:])
    za = _dot(h, w1_ref[:, :swa_w])
    cq = zl[:, :Q_LORA]
    ckv = zl[:, Q_LORA:Q_LORA + KV_LORA]
    kr = zl[:, Q_LORA + KV_LORA:]
    qa = za[:, :A_Q_W]
    ka = za[:, A_Q_W:A_Q_W + A_KV_W]
    va = za[:, A_Q_W + A_KV_W:]

    qat_ref[0] = (qa * (LOG2E / math.sqrt(A_HEAD_DIM))).T.astype(BF16)
    ka_ref[0] = ka.astype(BF16)
    vat_ref[0] = va.T.astype(BF16)

    cqn = _rms(cq, qn_ref[...]).astype(BF16)
    qb = _dot(cqn, wuq_ref[...]) * b_scale
    ckvn = _rms(ckv, kvn_ref[...]).astype(BF16)
    kb = _dot(ckvn, wk_ref[...])
    vb = _dot(ckvn, wv_ref[...])

    ang = pos_ref[0].astype(F32) * invf_ref[...]
    cos, sin = jnp.cos(ang), jnp.sin(ang)
    half = ROPE_DIM // 2
    r0, r1, r2 = NOPE_DIM, NOPE_DIM + half, NOPE_DIM + ROPE_DIM

    def rope_t(t):
        t1, t2 = t[r0:r1], t[r1:r2]
        return jnp.concatenate(
            [t[:r0], t1 * cos - t2 * sin, t2 * cos + t1 * sin, t[r2:]], axis=0)

    tq = qbt_ref.shape[4]
    for hd in range(B_HEADS):
        qt = rope_t(qb[:, hd * HEAD_PAD:(hd + 1) * HEAD_PAD].T).astype(BF16)
        for s in range(tm // tq):
            qbt_ref[0, hd, s] = qt[:, s * tq:(s + 1) * tq]

    kr_rot = rope_t(kr.T).T
    for hd in range(B_HEADS):
        kb_ref[0, hd] = (kb[:, hd * HEAD_PAD:(hd + 1) * HEAD_PAD] + kr_rot).astype(BF16)

    vbt = vb.T.astype(BF16)
    nsub = vbt_ref.shape[2]
    tk = vbt_ref.shape[4]
    for hd in range(B_HEADS):
        for s in range(nsub):
            vbt_ref[0, hd, s] = vbt[hd * V_DIM:(hd + 1) * V_DIM, s * tk:(s + 1) * tk]


def _in_proj(x, pos3, invf, g, w1, qn, kvn, wuq, wk, wv, *, tm, tq, tk):
    B, S, _ = x.shape
    nsub = tm // tk
    grid = (B, S // tm)
    full = lambda shape: pl.BlockSpec(shape, lambda b, i: (0,) * len(shape))
    out_shape = (
        jax.ShapeDtypeStruct((B, A_Q_W, S), BF16),
        jax.ShapeDtypeStruct((B, S, A_KV_W), BF16),
        jax.ShapeDtypeStruct((B, A_KV_W, S), BF16),
        jax.ShapeDtypeStruct((B, B_HEADS, S // tq, HEAD_PAD, tq), BF16),
        jax.ShapeDtypeStruct((B, B_HEADS, S, HEAD_PAD), BF16),
        jax.ShapeDtypeStruct((B, B_HEADS, S // tk, V_DIM, tk), BF16),
    )
    out_specs = (
        pl.BlockSpec((1, A_Q_W, tm), lambda b, i: (b, 0, i)),
        pl.BlockSpec((1, tm, A_KV_W), lambda b, i: (b, i, 0)),
        pl.BlockSpec((1, A_KV_W, tm), lambda b, i: (b, 0, i)),
        pl.BlockSpec((1, B_HEADS, tm // tq, HEAD_PAD, tq), lambda b, i: (b, 0, i, 0, 0)),
        pl.BlockSpec((1, B_HEADS, tm, HEAD_PAD), lambda b, i: (b, 0, i, 0)),
        pl.BlockSpec((1, B_HEADS, nsub, V_DIM, tk), lambda b, i: (b, 0, i, 0, 0)),
    )
    in_specs = [
        pl.BlockSpec((1, tm, D_MODEL), lambda b, i: (b, i, 0)),
        pl.BlockSpec((1, 1, tm), lambda b, i: (b, 0, i)),
        full(invf.shape), full(g.shape), full(w1.shape), full(qn.shape), full(kvn.shape),
        full(wuq.shape), full(wk.shape), full(wv.shape),
    ]
    return pl.pallas_call(
        functools.partial(_in_proj_kernel, b_scale=LOG2E / math.sqrt(B_QK_DIM)),
        out_shape=out_shape, grid=grid, in_specs=in_specs, out_specs=out_specs,
        compiler_params=pltpu.CompilerParams(
            dimension_semantics=("parallel", "parallel"),
            vmem_limit_bytes=V7X_VMEM_LIMIT),
        name="in_proj",
    )(x, pos3, invf, g, w1, qn, kvn, wuq, wk, wv)


def _bias_kernel(table_ref, bkt_ref, out_ref):
    bkt = bkt_ref[...]
    row = jax.lax.broadcasted_iota(jnp.int32, bkt.shape, 0)
    group = A_HEADS // A_KV_HEADS
    for hd in range(A_HEADS):
        acc = jnp.full(bkt.shape, -jnp.inf, F32)
        for b in range(N_BUCKETS):
            acc = jnp.where(bkt == b, table_ref[b, hd] * LOG2E, acc)
        g, lane = hd // group, (hd % group) * BLOCK
        out_ref[0, g, :, lane:lane + BLOCK] = acc
        out_ref[1, g, :, lane:lane + BLOCK] = jnp.where(row < BLOCK, -jnp.inf, acc)


def _bucket_table():
    kj = np.arange(2 * BLOCK)[:, None]
    qi = np.arange(BLOCK)[None, :]
    dist = BLOCK + qi - kj
    max_exact = N_BUCKETS // 2
    n = np.maximum(dist, 0)
    nf = np.maximum(n, 1).astype(np.float32)
    large = max_exact + (np.log(nf / max_exact) / math.log(MAX_DISTANCE / max_exact)
                         * (N_BUCKETS - max_exact)).astype(np.int32)
    large = np.minimum(large, N_BUCKETS - 1)
    bkt = np.where(n < max_exact, n, large)
    in_win = (dist >= 0) & (dist < WINDOW)
    return np.where(in_win, bkt, -1).astype(np.int32)


def _rel_bias(table):
    bkt = jnp.asarray(_bucket_table())
    return pl.pallas_call(
        _bias_kernel,
        out_shape=jax.ShapeDtypeStruct(
            (2, A_KV_HEADS, 2 * BLOCK, (A_HEADS // A_KV_HEADS) * BLOCK), F32),
        in_specs=[pl.BlockSpec(memory_space=pltpu.SMEM),
                  pl.BlockSpec(memory_space=pltpu.VMEM)],
        out_specs=pl.BlockSpec(memory_space=pltpu.VMEM),
        name="rel_bias",
    )(table, bkt)


def _swa_kernel(sink_ref, qt_ref, kc_ref, kp_ref, vc_ref, vp_ref, bias_ref, o_ref):
    i = pl.program_id(1)
    ta = qt_ref.shape[2]
    group = A_HEADS // A_KV_HEADS
    width = group * BLOCK
    kcat = jnp.concatenate([kp_ref[0], kc_ref[0]], axis=0)
    vcat = jnp.concatenate([vp_ref[0], vc_ref[0]], axis=1)
    ones = jnp.ones((ONES_ROWS, 2 * BLOCK), BF16)
    zpad = jnp.zeros((A_HEAD_DIM, width), BF16)
    units = [(r, g) for r in range(ta // BLOCK) for g in range(A_KV_HEADS)]

    def scores(r, g):
        kk = kcat[r * BLOCK:(r + 2) * BLOCK]
        qrow = jnp.concatenate(
            [qt_ref[0, hd * A_HEAD_DIM:(hd + 1) * A_HEAD_DIM, r * BLOCK:(r + 1) * BLOCK]
             for hd in range(g * group, (g + 1) * group)], axis=1)
        qpad = jnp.concatenate([qrow, zpad] if g == 0 else [zpad, qrow], axis=0)
        variant = jnp.where(i == 0, 1, 0) if r == 0 else 0
        return _dot(kk, qpad) + bias_ref[variant, g]

    def attend(r, g, s):
        sink = sink_ref[g]
        m = jnp.maximum(jnp.max(s, axis=0, keepdims=True), sink)
        p = jnp.exp2(s - m).astype(BF16)
        vv = vcat[g * A_HEAD_DIM:(g + 1) * A_HEAD_DIM, r * BLOCK:(r + 2) * BLOCK]
        acc = _dot(jnp.concatenate([vv, ones], axis=0), p)
        denom = acc[A_HEAD_DIM:A_HEAD_DIM + 1] + jnp.exp2(sink - m)
        return acc[:A_HEAD_DIM] / denom

    outs = {}
    pending = [scores(*units[0]), scores(*units[1])]
    for n, (r, g) in enumerate(units):
        s = pending.pop(0)
        if n + 2 < len(units):
            pending.append(scores(*units[n + 2]))
        outs[(r, g)] = attend(r, g, s)
    for r in range(ta // BLOCK):
        heads = [outs[(r, g)][:, hh * BLOCK:(hh + 1) * BLOCK]
                 for g in range(A_KV_HEADS) for hh in range(group)]
        o_ref[0, r * BLOCK:(r + 1) * BLOCK, :] = jnp.concatenate(heads, axis=0).T.astype(BF16)


def _swa(sink_rows, qat, ka, vat, bias, *, ta):
    B, _, S = qat.shape
    sub = ta // BLOCK
    prev = lambda i: jnp.maximum(i * sub - 1, 0)
    return pl.pallas_call(
        _swa_kernel,
        out_shape=jax.ShapeDtypeStruct((B, S, A_Q_W), BF16),
        grid=(B, S // ta),
        in_specs=[
            pl.BlockSpec(sink_rows.shape, lambda b, i: (0, 0, 0)),
            pl.BlockSpec((1, A_Q_W, ta), lambda b, i: (b, 0, i)),
            pl.BlockSpec((1, ta, A_KV_W), lambda b, i: (b, i, 0)),
            pl.BlockSpec((1, BLOCK, A_KV_W), lambda b, i: (b, prev(i), 0)),
            pl.BlockSpec((1, A_KV_W, ta), lambda b, i: (b, 0, i)),
            pl.BlockSpec((1, A_KV_W, BLOCK), lambda b, i: (b, 0, prev(i))),
            pl.BlockSpec(bias.shape, lambda b, i: (0, 0, 0, 0)),
        ],
        out_specs=pl.BlockSpec((1, ta, A_Q_W), lambda b, i: (b, i, 0)),
        compiler_params=pltpu.CompilerParams(
            dimension_semantics=("parallel", "parallel"),
            vmem_limit_bytes=V7X_VMEM_LIMIT),
        name="swa",
    )(sink_rows, qat, ka, ka, vat, vat, bias)


def _mla_kernel(qt_ref, k_ref, vt_ref, o_ref, *scratch):
    n_tiles = qt_ref.shape[2]
    tq = qt_ref.shape[4]
    tk = vt_ref.shape[4]
    nh = qt_ref.shape[1]
    assert tq == 2 * tk
    s_refs = [scratch[2 * hh:2 * hh + 2] for hh in range(nh)]
    p_refs = [scratch[2 * nh + 2 * hh:2 * nh + 2 * hh + 2] for hh in range(nh)]
    acc_refs = scratch[4 * nh:5 * nh]
    cm_ref = scratch[5 * nh]
    ones = jnp.ones((ONES_ROWS, tk), BF16)
    heads = range(nh)

    def qk(hh, blk, par, tile):
        kblk = k_ref[0, hh, pl.ds(pl.multiple_of(blk * tk, tk), tk), :]
        s = _dot(kblk, qt_ref[0, hh, tile])
        s_refs[hh][par][...] = s
        return jnp.max(s, axis=0, keepdims=True)

    def first_scores(tile):
        for hh in heads:
            for par in range(2):
                cm_ref[2 * hh + par, 0:1, :] = qk(hh, par, par, tile)

    def softmax(hh, par, m, cmax):
        m_new = jnp.maximum(m, cmax)
        p_refs[hh][par][...] = jnp.exp2(s_refs[hh][par][...] - m_new).astype(BF16)
        return m_new, jnp.exp2(m - m_new)

    def pv(hh, blk, par, alpha):
        vaug = jnp.concatenate([vt_ref[0, hh, blk], ones], axis=0)
        acc_refs[hh][...] = alpha * acc_refs[hh][...] + _dot(vaug, p_refs[hh][par][...])

    def query_tile(i):
        def pair(jj, carry):
            ms, alphas, cm0, cm1 = carry
            a = 2 * jj
            ms, alphas_a = zip(*[softmax(hh, 0, ms[hh], cm0[hh]) for hh in heads])
            for hh in heads:
                pv(hh, jnp.maximum(a - 1, 0), 1, alphas[hh])
            cm0 = [qk(hh, a + 2, 0, i) for hh in heads]
            ms, alphas_b = zip(*[softmax(hh, 1, ms[hh], cm1[hh]) for hh in heads])
            for hh in heads:
                pv(hh, a, 0, alphas_a[hh])
            cm1 = [qk(hh, a + 3, 1, i) for hh in heads]
            return list(ms), list(alphas_b), cm0, cm1

        def quad(qq, carry):
            return pair(2 * qq + 1, pair(2 * qq, carry))

        for hh in heads:
            p_refs[hh][1][...] = jnp.zeros((tk, tq), BF16)
            acc_refs[hh][...] = jnp.zeros((V_DIM + ONES_ROWS, tq), F32)
        cm0 = [cm_ref[2 * hh, 0:1, :] for hh in heads]
        cm1 = [cm_ref[2 * hh + 1, 0:1, :] for hh in heads]
        ms = [jnp.full((1, tq), -jnp.inf, F32) for _ in heads]
        alphas = [jnp.ones((1, tq), F32) for _ in heads]
        carry = jax.lax.fori_loop(0, i // 2, quad, (ms, alphas, cm0, cm1))
        carry = jax.lax.cond(i % 2 == 1, lambda c: pair(i - 1, c), lambda c: c, carry)
        ms, alphas, cm0, _ = carry

        d0 = 2 * i
        delta = (jax.lax.broadcasted_iota(jnp.int32, (tk, tk), 0)
                 - jax.lax.broadcasted_iota(jnp.int32, (tk, tk), 1))
        thr = i * tq - d0 * tk
        lo, hi = slice(0, tk), slice(tk, tq)

        def masked_softmax(hh, par, cols, m):
            s = jnp.where(delta <= thr, s_refs[hh][par][:, cols], -jnp.inf)
            m_new = jnp.maximum(m, jnp.max(s, axis=0, keepdims=True))
            p_refs[hh][par][:, cols] = jnp.exp2(s - m_new).astype(BF16)
            return m_new, jnp.exp2(m - m_new)

        alphas_a = []
        for hh in heads:
            m_lo, a_lo = masked_softmax(hh, 0, lo, ms[hh][:, lo])
            m_hi = jnp.maximum(ms[hh][:, hi], cm0[hh][:, hi])
            p_refs[hh][0][:, hi] = jnp.exp2(s_refs[hh][0][:, hi] - m_hi).astype(BF16)
            a_hi = jnp.exp2(ms[hh][:, hi] - m_hi)
            ms[hh] = jnp.concatenate([m_lo, m_hi], axis=1)
            alphas_a.append(jnp.concatenate([a_lo, a_hi], axis=1))
        for hh in heads:
            pv(hh, jnp.maximum(d0 - 1, 0), 1, alphas[hh])
        alphas_b = [masked_softmax(hh, 1, hi, ms[hh][:, hi])[1] for hh in heads]
        first_scores(jnp.minimum(i + 1, n_tiles - 1))
        for hh in heads:
            pv(hh, d0, 0, alphas_a[hh])
        for hh in heads:
            vaug = jnp.concatenate([vt_ref[0, hh, d0 + 1], ones], axis=0)
            acc_refs[hh][:, hi] = (alphas_b[hh] * acc_refs[hh][:, hi]
                                   + _dot(vaug, p_refs[hh][1][:, hi]))
        outs = [acc_refs[hh][:V_DIM, :] / acc_refs[hh][V_DIM:V_DIM + 1, :] for hh in heads]
        rows = pl.ds(pl.multiple_of(i * tq, tq), tq)
        o_ref[0, rows, :] = jnp.concatenate(outs, axis=0).T.astype(BF16)

    first_scores(0)

    def tile_step(i, carry):
        query_tile(i)
        return carry

    jax.lax.fori_loop(0, n_tiles, tile_step, 0)


def _mla(qbt, kb, vbt, *, nh=2):
    B, H, n_tiles, _, tq = qbt.shape
    S = n_tiles * tq
    tk = vbt.shape[4]
    return pl.pallas_call(
        _mla_kernel,
        out_shape=jax.ShapeDtypeStruct((B, S, H * V_DIM), BF16),
        grid=(B, H // nh),
        in_specs=[
            pl.BlockSpec((1, nh, n_tiles, HEAD_PAD, tq), lambda b, hp: (b, hp, 0, 0, 0)),
            pl.BlockSpec((1, nh, S, HEAD_PAD), lambda b, hp: (b, hp, 0, 0)),
            pl.BlockSpec((1, nh, S // tk, V_DIM, tk), lambda b, hp: (b, hp, 0, 0, 0)),
        ],
        out_specs=pl.BlockSpec((1, S, nh * V_DIM), lambda b, hp: (b, 0, hp)),
        scratch_shapes=([pltpu.VMEM((tk, tq), F32)] * (2 * nh)
                        + [pltpu.VMEM((tk, tq), BF16)] * (2 * nh)
                        + [pltpu.VMEM((V_DIM + ONES_ROWS, tq), F32)] * nh
                        + [pltpu.VMEM((2 * nh, 8, tq), F32)]),
        compiler_params=pltpu.CompilerParams(
            dimension_semantics=("parallel", "parallel"),
            vmem_limit_bytes=V7X_VMEM_LIMIT),
        name="mla",
    )(qbt, kb, vbt)


def _merge_kernel(x_ref, oa_ref, ob_ref, g_ref, wg_ref, wa_ref, wb_ref, wo_ref, o_ref):
    x = x_ref[0]
    h = _rms(x, g_ref[...]).astype(BF16)
    gates = jax.nn.sigmoid(_dot(h, wg_ref[...]))
    ya = _dot(oa_ref[0], wa_ref[...])
    yb = _dot(ob_ref[0], wb_ref[...])
    merged = gates[:, :D_MODEL] * ya + gates[:, D_MODEL:] * yb
    o_ref[0] = x + _dot(merged.astype(BF16), wo_ref[...])


def _merge(x, oa, ob, g, wg, wa, wb, wo, *, tm):
    B, S, _ = x.shape
    full = lambda shape: pl.BlockSpec(shape, lambda b, i: (0,) * len(shape))
    tok = lambda w: pl.BlockSpec((1, tm, w), lambda b, i: (b, i, 0))
    return pl.pallas_call(
        _merge_kernel,
        out_shape=jax.ShapeDtypeStruct(x.shape, F32),
        grid=(B, S // tm),
        in_specs=[tok(D_MODEL), tok(A_Q_W), tok(B_HEADS * V_DIM), full(g.shape),
                  full(wg.shape), full(wa.shape), full(wb.shape), full(wo.shape)],
        out_specs=tok(D_MODEL),
        compiler_params=pltpu.CompilerParams(
            dimension_semantics=("parallel", "parallel"),
            vmem_limit_bytes=V7X_VMEM_LIMIT),
        name="merge",
    )(x, oa, ob, g, wg, wa, wb, wo)


def _mlp_kernel(x_ref, g_ref, w1_ref, w2_ref, gf_ref, o_ref, *, final):
    x = x_ref[0]
    h = _rms(x, g_ref[...]).astype(BF16)
    u = jnp.square(jnp.maximum(_dot(h, w1_ref[...]), 0.0)).astype(BF16)
    y = x + _dot(u, w2_ref[...])
    if final:
        y = _rms(y, gf_ref[...])
    o_ref[0] = y


def _mlp(x, g, w1, w2, gf, *, tm, final):
    B, S, _ = x.shape
    full = lambda shape: pl.BlockSpec(shape, lambda b, i: (0,) * len(shape))
    tok = pl.BlockSpec((1, tm, D_MODEL), lambda b, i: (b, i, 0))
    return pl.pallas_call(
        functools.partial(_mlp_kernel, final=final),
        out_shape=jax.ShapeDtypeStruct(x.shape, F32),
        grid=(B, S // tm),
        in_specs=[tok, full(g.shape), full(w1.shape), full(w2.shape), full(gf.shape)],
        out_specs=tok,
        compiler_params=pltpu.CompilerParams(
            dimension_semantics=("parallel", "parallel"),
            vmem_limit_bytes=V7X_VMEM_LIMIT),
        name="mlp_final" if final else "mlp",
    )(x, g, w1, w2, gf)


def _prep_layer(w_in, w_uq, w_ukv):
    o = 0
    wqa = w_in[:, o:o + A_Q_W]; o += A_Q_W
    wka = w_in[:, o:o + A_KV_W]; o += A_KV_W
    wva = w_in[:, o:o + A_KV_W]; o += A_KV_W
    wcq = w_in[:, o:o + Q_LORA]; o += Q_LORA
    wckv = w_in[:, o:o + KV_LORA]; o += KV_LORA
    wkr = w_in[:, o:o + ROPE_DIM]; o += ROPE_DIM
    wg = w_in[:, o:]
    wkr_pad = jnp.pad(wkr, ((0, 0), (NOPE_DIM, HEAD_PAD - NOPE_DIM - ROPE_DIM)))
    w1 = jnp.concatenate([wqa, wka, wva, wcq, wckv, wkr_pad], axis=1).astype(BF16)
    wuq = w_uq.reshape(Q_LORA, B_HEADS, B_QK_DIM)
    wuq = jnp.pad(wuq, ((0, 0), (0, 0), (0, HEAD_PAD - B_QK_DIM)))
    wuq = wuq.reshape(Q_LORA, B_HEADS * HEAD_PAD).astype(BF16)
    wukv = w_ukv.reshape(KV_LORA, B_HEADS, NOPE_DIM + V_DIM)
    wk = jnp.pad(wukv[:, :, :NOPE_DIM], ((0, 0), (0, 0), (0, HEAD_PAD - NOPE_DIM)))
    wk = wk.reshape(KV_LORA, B_HEADS * HEAD_PAD).astype(BF16)
    wv = wukv[:, :, NOPE_DIM:].reshape(KV_LORA, B_HEADS * V_DIM).astype(BF16)
    return w1, wg.astype(BF16), wuq, wk, wv


def kernel(x, positions, rel_bias_table, norm_mix, w_in, attn_sinks, q_norm, kv_norm,
           w_uq, w_ukv, w_branch_a, w_branch_b, w_out, norm_mlp, w_ff1, w_ff2, norm_final):
    B, S, _ = x.shape
    depth = w_in.shape[0]
    tm = min(TOK_TILE, S)
    t = min(MLA_TILE, S)
    tk = MLA_KEYS
    ta = min(SWA_TILE, S)
    tmg = min(MERGE_TILE, S)
    assert S % tm == 0 and S % ta == 0 and S % tmg == 0
    assert tm % t == 0 and t == 2 * tk and ta % BLOCK == 0

    half = ROPE_DIM // 2
    inv_freq = ROPE_THETA ** (-jnp.arange(half, dtype=F32) / half)
    invf = inv_freq.reshape(half, 1)
    pos3 = positions.reshape(B, 1, S)
    bias = _rel_bias(rel_bias_table.astype(F32))
    row = lambda v: v.reshape(1, -1).astype(F32)

    for l in range(depth):
        w1, wg, wuq, wk, wv = _prep_layer(w_in[l], w_uq[l], w_ukv[l])
        qat, ka, vat, qbt, kb, vbt = _in_proj(
            x, pos3, invf, row(norm_mix[l]), w1, row(q_norm[l]), row(kv_norm[l]),
            wuq, wk, wv, tm=tm, tq=t, tk=tk)
        sink_rows = jnp.repeat(attn_sinks[l].astype(F32).reshape(A_KV_HEADS, 1, -1) * LOG2E,
                               BLOCK, axis=2)
        oa = _swa(sink_rows, qat, ka, vat, bias, ta=ta)
        ob = _mla(qbt, kb, vbt)
        x = _merge(x, oa, ob, row(norm_mix[l]), wg, w_branch_a[l].astype(BF16),
                   w_branch_b[l].astype(BF16), w_out[l].astype(BF16), tm=tmg)
        x = _mlp(x, row(norm_mlp[l]), w_ff1[l].astype(BF16), w_ff2[l].astype(BF16),
                 row(norm_final), tm=tm, final=(l == depth - 1))
    return x
```
